```python
import math
import jax, jax.numpy as jnp
from jax import lax
import numpy as np

D_MODEL = 1024
BATCH = 32
SEQ = 2048
DEPTH = 1

CHUNK = 64
Q_BLOCK = 128
HEAD_DIM = 64
ROPE_THETA = 10000.0
EPS = 1e-6
DIFF_HEADS = 4
DIFF_WIDTH = DIFF_HEADS * 2 * HEAD_DIM
DSA_HEADS = 8
DSA_WIDTH = DSA_HEADS * HEAD_DIM
IDX_HEADS = 8
DSA_TOPK_MAX = 256
N_BRANCH = 2
PEER_HEADS = 8
PEER_NKEYS = 128
PEER_EXPERTS = PEER_NKEYS * PEER_NKEYS
PEER_DKEY = 256
PEER_TOPK = 16
PEER_BLOCK = 128
IN_SIZES = (
    DIFF_WIDTH,
    DIFF_WIDTH,
    DIFF_WIDTH,
    DSA_WIDTH,
    HEAD_DIM,
    HEAD_DIM,
    IDX_HEADS * HEAD_DIM,
    HEAD_DIM,
    IDX_HEADS,
    N_BRANCH * D_MODEL,
)
IN_COLS = sum(IN_SIZES)

kernel_name = "chunk_causal_hybrid_diff_dsa_peer_block"


def rmsnorm(x, g):
    x32 = x.astype(jnp.float32)
    y = x32 * lax.rsqrt(jnp.mean(x32 * x32, axis=-1, keepdims=True) + EPS)
    return (y * g.astype(jnp.float32)).astype(x.dtype)


def rope_cos_sin(positions, dim):
    inv = ROPE_THETA ** (-(jnp.arange(0, dim, 2, dtype=jnp.float32) / dim))
    ang = positions.astype(jnp.float32)[..., None] * inv
    return jnp.cos(ang), jnp.sin(ang)


def apply_rope(x, cos, sin):
    x32 = x.astype(jnp.float32)
    half = x.shape[-1] // 2
    x1, x2 = x32[..., :half], x32[..., half:]
    c = cos[:, :, None, :]
    s = sin[:, :, None, :]
    return jnp.concatenate([x1 * c - x2 * s, x2 * c + x1 * s], axis=-1).astype(x.dtype)


def diff_attention(q, k, v, lam, out_g, lambda_init):
    B, T, H, _, d = q.shape
    nb = T // Q_BLOCK
    qb = q.reshape(B, nb, Q_BLOCK, H, 2, d).transpose(1, 0, 2, 3, 4, 5)
    k_chunk = jnp.arange(T) // CHUNK
    scale = d ** -0.5

    def block(args):
        i, q_i = args
        q_chunk = (i * Q_BLOCK + jnp.arange(Q_BLOCK)) // CHUNK
        mask = k_chunk[None, :] <= q_chunk[:, None]
        s = jnp.einsum('bqhmd,bkhmd->bhmqk', q_i, k).astype(jnp.float32) * scale
        s = jnp.where(mask, s, -jnp.inf)
        p = jax.nn.softmax(s, axis=-1)
        p = p[:, :, 0] - lam * p[:, :, 1]
        return jnp.einsum('bhqk,bkhd->bqhd', p.astype(v.dtype), v)

    o = lax.map(block, (jnp.arange(nb), qb))
    o = o.transpose(1, 0, 2, 3, 4).reshape(B, T, H, 2 * d)
    o = rmsnorm(o, out_g) * (1.0 - lambda_init)
    return o.reshape(B, T, H * 2 * d)


def dsa_attention(q, k, v, q_idx, k_idx, w_idx):
    B, T, H, d = q.shape
    topk = min(DSA_TOPK_MAX, T // 4)
    nb = T // Q_BLOCK
    to_blocks = lambda a: a.reshape((B, nb, Q_BLOCK) + a.shape[2:]).swapaxes(0, 1)
    k_chunk = jnp.arange(T) // CHUNK
    scale = d ** -0.5
    idx_scale = (IDX_HEADS ** -0.5) * (HEAD_DIM ** -0.5)
    gather = jax.vmap(lambda table, idx: table[idx])

    def block(args):
        i, q_b, qi_b, wi_b = args
        q_chunk = (i * Q_BLOCK + jnp.arange(Q_BLOCK)) // CHUNK
        mask = k_chunk[None, :] <= q_chunk[:, None]
        dots = jnp.einsum('bqhd,bkd->bqhk', qi_b, k_idx).astype(jnp.float32)
        score = jnp.einsum('bqhk,bqh->bqk', jax.nn.relu(dots), wi_b.astype(jnp.float32)) * idx_scale
        score = jnp.where(mask[None], score, -jnp.inf)
        _, sel = lax.top_k(score, topk)
        valid = (sel // CHUNK) <= q_chunk[None, :, None]
        k_sel = gather(k, sel)
        v_sel = gather(v, sel)
        s = jnp.einsum('bqhd,bqkd->bqhk', q_b, k_sel).astype(jnp.float32) * scale
        s = jnp.where(valid[:, :, None, :], s, -jnp.inf)
        p = jax.nn.softmax(s, axis=-1)
        o = jnp.einsum('bqhk,bqkd->bqhd', p.astype(v.dtype), v_sel)
        return o.reshape(B, Q_BLOCK, H * d)

    o = lax.map(block, (jnp.arange(nb), to_blocks(q), to_blocks(q_idx), to_blocks(w_idx)))
    return o.swapaxes(0, 1).reshape(B, T, H * d)


def peer(h, w_q, sub_keys, expert_u, expert_v):
    B, T, D = h.shape
    n_tok = B * T
    hb = h.reshape(n_tok // PEER_BLOCK, PEER_BLOCK, D)

    def block(xb):
        q = (xb @ w_q).reshape(PEER_BLOCK, PEER_HEADS, 2, PEER_DKEY // 2)
        s = jnp.einsum('nhpd,hpkd->nhpk', q, sub_keys).astype(jnp.float32)
        v1, i1 = lax.top_k(s[:, :, 0], PEER_TOPK)
        v2, i2 = lax.top_k(s[:, :, 1], PEER_TOPK)
        cand = (v1[..., :, None] + v2[..., None, :]).reshape(PEER_BLOCK, PEER_HEADS, PEER_TOPK * PEER_TOPK)
        cidx = (i1[..., :, None] * PEER_NKEYS + i2[..., None, :]).reshape(PEER_BLOCK, PEER_HEADS, PEER_TOPK * PEER_TOPK)
        sc, pos = lax.top_k(cand, PEER_TOPK)
        eidx = jnp.take_along_axis(cidx, pos, axis=-1)
        g = jax.nn.softmax(sc, axis=-1)
        u_sel = expert_u[eidx]
        a = jax.nn.gelu(jnp.einsum('nd,nhkd->nhk', xb, u_sel), approximate=False)
        wgt = (g * a.astype(jnp.float32)).astype(xb.dtype)
        return jnp.einsum('nhk,nhkd->nd', wgt, expert_v[eidx])

    return lax.map(block, hb).reshape(B, T, D)


def setup_inputs(seed: int = 0) -> dict:
    key = jax.random.key(seed)
    ks = jax.random.split(key, 24)
    f32 = jnp.float32
    L, D = DEPTH, D_MODEL

    def nrm(k, shape, scale):
        return jax.random.normal(k, shape, f32) * scale

    def gain(k, shape):
        return 1.0 + 0.02 * jax.random.normal(k, shape, f32)

    x = nrm(ks[0], (BATCH, SEQ, D), 1.0)
    c = nrm(ks[1], (BATCH, D), 1.0)
    offs = jax.random.randint(ks[2], (BATCH, 1), 0, 64, dtype=jnp.int32) * CHUNK
    positions = offs + jnp.arange(SEQ, dtype=jnp.int32)[None, :]
    return {
        'x': x,
        'c': c,
        'positions': positions,
        'w_ada': nrm(ks[3], (L, D, 6 * D), D ** -0.5),
        'b_ada': nrm(ks[4], (L, 6 * D), 0.01),
        'norm1_g': gain(ks[5], (L, D)),
        'w_in': nrm(ks[6], (L, D, IN_COLS), D ** -0.5),
        'diff_q_g': gain(ks[7], (L, HEAD_DIM)),
        'diff_k_g': gain(ks[8], (L, HEAD_DIM)),
        'diff_lam_q1': nrm(ks[9], (L, HEAD_DIM), 0.1),
        'diff_lam_k1': nrm(ks[10], (L, HEAD_DIM), 0.1),
        'diff_lam_q2': nrm(ks[11], (L, HEAD_DIM), 0.1),
        'diff_lam_k2': nrm(ks[12], (L, HEAD_DIM), 0.1),
        'diff_out_g': gain(ks[13], (L, 2 * HEAD_DIM)),
        'dsa_q_g': gain(ks[14], (L, HEAD_DIM)),
        'dsa_k_g': gain(ks[15], (L, HEAD_DIM)),
        'w_branch_a': nrm(ks[16], (L, DIFF_WIDTH, D), DIFF_WIDTH ** -0.5),
        'w_branch_b': nrm(ks[17], (L, DSA_WIDTH, D), DSA_WIDTH ** -0.5),
        'w_out': nrm(ks[18], (L, D, D), D ** -0.5),
        'norm2_g': gain(ks[19], (L, D)),
        'peer_w_q': nrm(ks[20], (L, D, PEER_HEADS * PEER_DKEY), D ** -0.5),
        'peer_sub_keys': nrm(ks[21], (L, PEER_HEADS, 2, PEER_NKEYS, PEER_DKEY // 2), (PEER_DKEY // 2) ** -0.5),
        'peer_u': nrm(ks[22], (L, PEER_EXPERTS, D), D ** -0.5),
        'peer_v': nrm(ks[23], (L, PEER_EXPERTS, D), PEER_HEADS ** -0.5),
    }


def reference(x, c, positions, w_ada, b_ada, norm1_g, w_in, diff_q_g, diff_k_g,
              diff_lam_q1, diff_lam_k1, diff_lam_q2, diff_lam_k2, diff_out_g,
              dsa_q_g, dsa_k_g, w_branch_a, w_branch_b, w_out, norm2_g,
              peer_w_q, peer_sub_keys, peer_u, peer_v):
    B, T, D = x.shape
    cos, sin = rope_cos_sin(positions, HEAD_DIM)
    offsets = np.cumsum(IN_SIZES)[:-1].tolist()
    for l in range(DEPTH):
        lambda_init = 0.8 - 0.6 * math.exp(-0.3 * l)
        mod = jax.nn.silu(c) @ w_ada[l] + b_ada[l]
        sh1, sc1, g1, sh2, sc2, g2 = jnp.split(mod, 6, axis=-1)

        h = rmsnorm(x, norm1_g[l]) * (1.0 + sc1[:, None, :]) + sh1[:, None, :]
        proj = h @ w_in[l]
        (dq, dk, dv, sq, sk, sv, iq, ik, iw, gates) = jnp.split(proj, offsets, axis=-1)

        dq = apply_rope(rmsnorm(dq.reshape(B, T, DIFF_HEADS * 2, HEAD_DIM), diff_q_g[l]), cos, sin)
        dk = apply_rope(rmsnorm(dk.reshape(B, T, DIFF_HEADS * 2, HEAD_DIM), diff_k_g[l]), cos, sin)
        lam = (jnp.exp(jnp.sum(diff_lam_q1[l] * diff_lam_k1[l]).astype(jnp.float32))
               - jnp.exp(jnp.sum(diff_lam_q2[l] * diff_lam_k2[l]).astype(jnp.float32))
               + lambda_init)
        o_a = diff_attention(dq.reshape(B, T, DIFF_HEADS, 2, HEAD_DIM),
                             dk.reshape(B, T, DIFF_HEADS, 2, HEAD_DIM),
                             dv.reshape(B, T, DIFF_HEADS, 2 * HEAD_DIM),
                             lam, diff_out_g[l], lambda_init)

        sq = apply_rope(rmsnorm(sq.reshape(B, T, DSA_HEADS, HEAD_DIM), dsa_q_g[l]), cos, sin)
        sk = apply_rope(rmsnorm(sk.reshape(B, T, 1, HEAD_DIM), dsa_k_g[l]), cos, sin)[:, :, 0]
        iq = apply_rope(iq.reshape(B, T, IDX_HEADS, HEAD_DIM), cos, sin)
        ik = apply_rope(ik.reshape(B, T, 1, HEAD_DIM), cos, sin)[:, :, 0]
        o_b = dsa_attention(sq, sk, sv, iq, ik, iw)

        gates = jax.nn.sigmoid(gates.reshape(B, T, N_BRANCH, D))
        merged = gates[:, :, 0] * (o_a @ w_branch_a[l]) + gates[:, :, 1] * (o_b @ w_branch_b[l])
        x = x + g1[:, None, :] * (merged @ w_out[l])

        h2 = rmsnorm(x, norm2_g[l]) * (1.0 + sc2[:, None, :]) + sh2[:, None, :]
        x = x + g2[:, None, :] * peer(h2, peer_w_q[l], peer_sub_keys[l], peer_u[l], peer_v[l])
    return x
```

```python
import functools
import math

import numpy as np
import jax
import jax.numpy as jnp
from jax import lax
from jax.experimental import pallas as pl
from jax.experimental.pallas import tpu as pltpu

F32 = jnp.float32
BF16 = jnp.bfloat16
I32 = jnp.int32

LANES = 128
HEAD_DIM = 64
HALF = HEAD_DIM // 2
CHUNK = 64
ROPE_THETA = 10000.0
EPS = 1e-6
DIFF_HEADS = 4
DSA_HEADS = 8
IDX_HEADS = 8
DSA_TOPK_MAX = 256
PEER_HEADS = 8
PEER_NKEYS = 128
PEER_TOPK = 16
PEER_PAIRS = PEER_HEADS * PEER_TOPK
LAMBDA_INIT = 0.8 - 0.6 * math.exp(-0.3 * 0)
NEG = -1e30
VMEM_LIMIT = 56 * 1024 * 1024

C_DQ, C_DK, C_DV, C_SQ, C_IQ = 0, 512, 1024, 1536, 2048
C_SK, C_IK, C_SV, C_IW, C_GATE = 2560, 2688, 2816, 2944, 3072
N_COLS = C_GATE + 2048


def _nt(a, b):
    return lax.dot_general(a, b, (((1,), (1,)), ((), ())), preferred_element_type=F32)


def _split_bf16(x):
    hi = x.astype(BF16)
    lo = (x - hi.astype(F32)).astype(BF16)
    return hi, lo


def _dot2(x, w):
    hi, lo = _split_bf16(x)
    return (jnp.dot(hi, w, preferred_element_type=F32)
            + jnp.dot(lo, w, preferred_element_type=F32))


def _ada_kernel(c_ref, w_ref, b_ref, o_ref):
    c = c_ref[...]
    s = c * jax.nn.sigmoid(c)
    o_ref[...] = jnp.dot(s, w_ref[...], preferred_element_type=F32,
                         precision=lax.Precision.HIGHEST) + b_ref[...]


def _ada(c, w, b):
    B, D = c.shape
    n = w.shape[1]
    bn = 1024
    return pl.pallas_call(
        _ada_kernel,
        grid=(n // bn,),
        in_specs=[pl.BlockSpec((B, D), lambda j: (0, 0)),
                  pl.BlockSpec((D, bn), lambda j: (0, j)),
                  pl.BlockSpec((1, bn), lambda j: (0, j))],
        out_specs=pl.BlockSpec((B, bn), lambda j: (0, j)),
        out_shape=jax.ShapeDtypeStruct((B, n), F32),
        compiler_params=pltpu.CompilerParams(dimension_semantics=("parallel",),
                                             vmem_limit_bytes=VMEM_LIMIT),
        name="ada",
    )(c, w, b.reshape(1, n))


def _rope128(y, cos, sin_signed, first_half):
    partner = jnp.where(first_half, pltpu.roll(y, LANES - HALF, 1), pltpu.roll(y, HALF, 1))
    return y * cos + partner * sin_signed


def _group_ms(y, gmat):
    return _dot2(y * y, gmat)


def _inproj_kernel(x_ref, sc_ref, sh_ref, n1_ref, pos_ref, inv_ref, sgn_ref, gq_ref, gk_ref,
                   gsq_ref, gsk_ref, gmat_ref, w_ref,
                   dq_ref, dk_ref, dv_ref, sq_ref, iq_ref, sk_ref, ik_ref, sv_ref, iw_ref, gate_ref):
    x = x_ref[0]
    ms = jnp.mean(x * x, axis=-1, keepdims=True)
    h = x * lax.rsqrt(ms + EPS) * n1_ref[...] * (1.0 + sc_ref[0]) + sh_ref[0]
    hb = h.astype(BF16)
    tm = x.shape[0]

    ang = pos_ref[0].astype(F32) * inv_ref[...]
    cos = jnp.cos(ang)
    sin = jnp.sin(ang) * sgn_ref[...]
    lane = lax.broadcasted_iota(I32, (tm, LANES), 1)
    first_half = (lane % HEAD_DIM) < HALF
    gmat = gmat_ref[...]

    def proj(c0, n):
        return jnp.dot(hb, w_ref[:, c0:c0 + n], preferred_element_type=F32)

    def normed_rope(c0, nblk, gain_ref, out_ref, scale):
        for i in range(nblk):
            y = proj(c0 + i * LANES, LANES)
            if gain_ref is not None:
                y = y * lax.rsqrt(_group_ms(y, gmat) + EPS) * gain_ref[:, i * LANES:(i + 1) * LANES]
            y = _rope128(y, cos, sin, first_half)
            if scale != 1.0:
                y = y * scale
            out_ref[0, :, i * LANES:(i + 1) * LANES] = y.astype(out_ref.dtype)

    qscale = HEAD_DIM ** -0.5
    normed_rope(C_DQ, 4, gq_ref, dq_ref, qscale)
    normed_rope(C_DK, 4, gk_ref, dk_ref, 1.0)
    dv_ref[0] = proj(C_DV, 512).astype(dv_ref.dtype)
    normed_rope(C_SQ, 4, gsq_ref, sq_ref, qscale)
    normed_rope(C_IQ, 4, None, iq_ref, 1.0)
    normed_rope(C_SK, 1, gsk_ref, sk_ref, 1.0)
    normed_rope(C_IK, 1, None, ik_ref, 1.0)
    sv_ref[0] = proj(C_SV, LANES).astype(sv_ref.dtype)
    iw_ref[0] = proj(C_IW, LANES)
    for i in range(4):
        g = proj(C_GATE + i * 512, 512)
        gate_ref[0, :, i * 512:(i + 1) * 512] = jax.nn.sigmoid(g).astype(gate_ref.dtype)


def _inproj(x, sc1, sh1, n1g, pos, consts, w_perm, tm):
    B, T, D = x.shape
    inv128, sgn128, gq, gk, gsq, gsk, gmat = consts
    tok = lambda n, dt: jax.ShapeDtypeStruct((B, T, n), dt)
    blk = lambda n: pl.BlockSpec((1, tm, n), lambda b, i: (b, i, 0))
    cst = lambda a: pl.BlockSpec(a.shape, lambda b, i: (0,) * a.ndim)
    mod = pl.BlockSpec((1, 1, D), lambda b, i: (b, 0, 0))
    return pl.pallas_call(
        _inproj_kernel,
        grid=(B, T // tm),
        in_specs=[blk(D), mod, mod, cst(n1g), blk(1), cst(inv128), cst(sgn128), cst(gq), cst(gk),
                  cst(gsq), cst(gsk), cst(gmat), cst(w_perm)],
        out_specs=[blk(512), blk(512), blk(512), blk(512), blk(512), blk(128), blk(128), blk(128),
                   blk(128), blk(2048)],
        out_shape=[tok(512, BF16), tok(512, BF16), tok(512, BF16), tok(512, BF16), tok(512, BF16),
                   tok(128, BF16), tok(128, BF16), tok(128, BF16), tok(128, F32), tok(2048, BF16)],
        compiler_params=pltpu.CompilerParams(dimension_semantics=("parallel", "parallel"),
                                             vmem_limit_bytes=VMEM_LIMIT),
        name="inproj",
    )(x, sc1, sh1, n1g, pos, inv128, sgn128, gq, gk, gsq, gsk, gmat, w_perm)


def _diff_kernel(q_ref, k_ref, v_ref, lq1_ref, lk1_ref, lq2_ref, lk2_ref, og_ref, o_ref, *, tq, tk):
    qi = pl.program_id(2)
    q = q_ref[0]
    lane = lax.broadcasted_iota(I32, (tq, LANES), 1)
    zero = jnp.zeros_like(q)
    q2 = jnp.concatenate([jnp.where(lane < HEAD_DIM, q, zero),
                          jnp.where(lane >= HEAD_DIM, q, zero)], axis=0)
    row = lax.broadcasted_iota(I32, (2 * tq, tk), 0)
    col = lax.broadcasted_iota(I32, (2 * tq, tk), 1)
    q_chunk = (qi * tq + jnp.where(row >= tq, row - tq, row)) // CHUNK

    def body(j, carry):
        m, l, acc = carry
        start = pl.multiple_of(j * tk, tk)
        kt = k_ref[0, pl.ds(start, tk), :]
        vt = v_ref[0, pl.ds(start, tk), :]
        s = _nt(q2, kt)
        ok = (j * tk + col) // CHUNK <= q_chunk
        s = jnp.where(ok, s, NEG)
        m_new = jnp.maximum(m, jnp.max(s, axis=-1, keepdims=True))
        alpha = jnp.exp(m - m_new)
        p = jnp.where(ok, jnp.exp(s - m_new), 0.0)
        l = alpha * l + jnp.sum(p, axis=-1, keepdims=True)
        acc = alpha * acc + jnp.dot(p.astype(BF16), vt, preferred_element_type=F32)
        return m_new, l, acc

    init = (jnp.full((2 * tq, 1), NEG, F32), jnp.zeros((2 * tq, 1), F32),
            jnp.zeros((2 * tq, LANES), F32))
    n_kv = (qi * tq + tq + tk - 1) // tk
    m, l, acc = lax.fori_loop(0, n_kv, body, init)
    o = acc / l
    lam = (jnp.exp(jnp.sum(lq1_ref[...] * lk1_ref[...], axis=-1, keepdims=True))
           - jnp.exp(jnp.sum(lq2_ref[...] * lk2_ref[...], axis=-1, keepdims=True)) + LAMBDA_INIT)
    o = o[:tq] - lam * o[tq:]
    ms = jnp.mean(o * o, axis=-1, keepdims=True)
    o = o * lax.rsqrt(ms + EPS) * og_ref[...] * (1.0 - LAMBDA_INIT)
    o_ref[0] = o.astype(o_ref.dtype)


def _diff_attention(dq, dk, dv, lams, out_g, tq, tk):
    B, T, _ = dq.shape
    cst = lambda a: pl.BlockSpec(a.shape, lambda b, h, i: (0,) * a.ndim)
    return pl.pallas_call(
        functools.partial(_diff_kernel, tq=tq, tk=tk),
        grid=(B, DIFF_HEADS, T // tq),
        in_specs=[pl.BlockSpec((1, tq, LANES), lambda b, h, i: (b, i, h)),
                  pl.BlockSpec((1, T, LANES), lambda b, h, i: (b, 0, h)),
                  pl.BlockSpec((1, T, LANES), lambda b, h, i: (b, 0, h)),
                  cst(lams[0]), cst(lams[1]), cst(lams[2]), cst(lams[3]), cst(out_g)],
        out_specs=pl.BlockSpec((1, tq, LANES), lambda b, h, i: (b, i, h)),
        out_shape=jax.ShapeDtypeStruct((B, T, DIFF_HEADS * LANES), BF16),
        compiler_params=pltpu.CompilerParams(
            dimension_semantics=("parallel", "parallel", "parallel"), vmem_limit_bytes=VMEM_LIMIT),
        name="diff_attention",
    )(dq, dk, dv, *lams, out_g)


def _dsa_kernel(sq_ref, iq_ref, iw_ref, sk_ref, ik_ref, sv_ref, o_ref, key_ref, *, tq, tk, topk, t_bits):
    qi = pl.program_id(1)
    n_kv = (qi * tq + tq + tk - 1) // tk
    lane = lax.broadcasted_iota(I32, (tq, LANES), 1)
    lo_half = lane < HEAD_DIM
    row = lax.broadcasted_iota(I32, (tq, tk), 0)
    col = lax.broadcasted_iota(I32, (tq, tk), 1)
    q_chunk = (qi * tq + row) // CHUNK
    idx_scale = (IDX_HEADS ** -0.5) * (HEAD_DIM ** -0.5)
    key_neg_inf = jnp.int32(-2139095041)

    def head_slabs(ref):
        out = []
        for p in range(4):
            slab = ref[0, :, p * LANES:(p + 1) * LANES]
            zero = jnp.zeros_like(slab)
            out.append(jnp.where(lo_half, slab, zero))
            out.append(jnp.where(lo_half, zero, slab))
        return out

    iq_heads = head_slabs(iq_ref)
    iw = iw_ref[0]

    def score_body(j, carry):
        start = pl.multiple_of(j * tk, tk)
        ikt = ik_ref[0, pl.ds(start, tk), :]
        score = jnp.zeros((tq, tk), F32)
        for h in range(IDX_HEADS):
            d = _nt(iq_heads[h], ikt)
            score = score + jnp.maximum(d, 0.0) * iw[:, h:h + 1]
        score = score * idx_scale + 0.0
        ok = (j * tk + col) // CHUNK <= q_chunk
        score = jnp.where(ok, score, -jnp.inf)
        bits = lax.bitcast_convert_type(score, I32)
        key_ref[j] = bits ^ ((bits >> 31) & jnp.int32(0x7FFFFFFF))
        return carry

    lax.fori_loop(0, n_kv, score_body, 0)

    def count(pred_fn):
        def body(j, c):
            kj = key_ref[j]
            hit = jnp.where(pred_fn(kj, j * tk + col), 1.0, 0.0)
            for b in range(tk // LANES):
                c = c + hit[:, b * LANES:(b + 1) * LANES]
            return c
        c = lax.fori_loop(0, n_kv, body, jnp.zeros((tq, LANES), F32))
        return jnp.sum(c, axis=-1, keepdims=True)

    def bit_body(it, t):
        cand = t + jnp.left_shift(jnp.int32(1), 31 - it)
        c = count(lambda kj, _: kj >= cand)
        return jnp.where(c >= float(topk), cand, t)

    thr = lax.fori_loop(0, 32, bit_body, jnp.full((tq, 1), jnp.iinfo(jnp.int32).min, I32))
    need = float(topk) - count(lambda kj, _: kj > thr)

    def tie_body(it, jmax):
        cand = jmax + jnp.left_shift(jnp.int32(1), t_bits - 1 - it)
        c = count(lambda kj, kidx: (kj == thr) & (kidx < cand))
        return jnp.where(c < need, cand, jmax)

    jmax = lax.fori_loop(0, t_bits, tie_body, jnp.zeros((tq, 1), I32))

    q_all = jnp.concatenate(head_slabs(sq_ref), axis=0)

    def attn_body(j, carry):
        m, l, acc = carry
        start = pl.multiple_of(j * tk, tk)
        kt = sk_ref[0, pl.ds(start, tk), :]
        vt = sv_ref[0, pl.ds(start, tk), :]
        kj = key_ref[j]
        kidx = j * tk + col
        sel = (kj > thr) | ((kj == thr) & (kidx <= jmax))
        sel = sel & (kj != key_neg_inf)
        sel1 = jnp.where(sel, 1.0, 0.0)
        sel8 = jnp.concatenate([sel1] * DSA_HEADS, axis=0)
        s = jnp.where(sel8 > 0.5, _nt(q_all, kt), NEG)
        m_new = jnp.maximum(m, jnp.max(s, axis=-1, keepdims=True))
        alpha = jnp.exp(m - m_new)
        p = jnp.exp(s - m_new) * sel8
        l = alpha * l + jnp.sum(p, axis=-1, keepdims=True)
        acc = alpha * acc + jnp.dot(p.astype(BF16), vt, preferred_element_type=F32)
        return m_new, l, acc

    rows = DSA_HEADS * tq
    init = (jnp.full((rows, 1), NEG, F32), jnp.zeros((rows, 1), F32), jnp.zeros((rows, LANES), F32))
    _, l, acc = lax.fori_loop(0, n_kv, attn_body, init)
    o = acc / l
    for p in range(4):
        even = o[(2 * p) * tq:(2 * p + 1) * tq]
        odd = o[(2 * p + 1) * tq:(2 * p + 2) * tq]
        o_ref[0, :, p * LANES:(p + 1) * LANES] = jnp.where(lo_half, even, odd).astype(o_ref.dtype)


def _dsa_attention(sq, iq, iw, skk, ikk, svv, tq, tk):
    B, T, _ = sq.shape
    topk = min(DSA_TOPK_MAX, T // 4)
    t_bits = max(1, (T - 1).bit_length())
    qblk = lambda n: pl.BlockSpec((1, tq, n), lambda b, i: (b, i, 0))
    kblk = pl.BlockSpec((1, T, LANES), lambda b, i: (b, 0, 0))
    return pl.pallas_call(
        functools.partial(_dsa_kernel, tq=tq, tk=tk, topk=topk, t_bits=t_bits),
        grid=(B, T // tq),
        in_specs=[qblk(512), qblk(512), qblk(LANES), kblk, kblk, kblk],
        out_specs=qblk(512),
        out_shape=jax.ShapeDtypeStruct((B, T, DSA_HEADS * HEAD_DIM), BF16),
        scratch_shapes=[pltpu.VMEM((T // tk, tq, tk), I32)],
        compiler_params=pltpu.CompilerParams(dimension_semantics=("parallel", "parallel"),
                                             vmem_limit_bytes=VMEM_LIMIT),
        name="dsa_attention",
    )(sq, iq, iw, skk, ikk, svv)


def _top_rows(s, k, payload=None):
    R, n = s.shape
    rid = lax.broadcasted_iota(I32, (R, n), 0).astype(F32)
    vals, ids, pay = [], [], []
    for _ in range(k):
        m = jnp.max(s, axis=0, keepdims=True)
        am = jnp.min(jnp.where(s == m, rid, float(R)), axis=0, keepdims=True)
        hit = rid == am
        vals.append(m)
        ids.append(am)
        if payload is not None:
            pay.append(jnp.sum(jnp.where(hit, payload, 0.0), axis=0, keepdims=True))
        s = jnp.where(hit, -jnp.inf, s)
    cat = lambda xs: jnp.concatenate(xs, axis=0)
    return cat(vals), cat(ids), (cat(pay) if payload is not None else None)


def _mix_kernel(oa_ref, ob_ref, gate_ref, x_ref, g1_ref, sc_ref, sh_ref, n2_ref, wa_ref, wb_ref, wo_ref,
                wq_ref, sub_ref, x1_ref, h2_ref, eidx_ref, gw_ref):
    ya = jnp.dot(oa_ref[0], wa_ref[...], preferred_element_type=F32)
    yb = jnp.dot(ob_ref[0], wb_ref[...], preferred_element_type=F32)
    D = ya.shape[1]
    merged = gate_ref[0, :, :D].astype(F32) * ya + gate_ref[0, :, D:].astype(F32) * yb
    x1 = x_ref[0] + g1_ref[0] * jnp.dot(merged.astype(BF16), wo_ref[...], preferred_element_type=F32)
    x1_ref[0] = x1
    ms = jnp.mean(x1 * x1, axis=-1, keepdims=True)
    h2 = x1 * lax.rsqrt(ms + EPS) * n2_ref[...] * (1.0 + sc_ref[0]) + sh_ref[0]
    h2_ref[0] = h2
    q = jnp.dot(h2.astype(BF16), wq_ref[...], preferred_element_type=F32)

    for h in range(PEER_HEADS):
        tops = []
        for p in range(2):
            hp = 2 * h + p
            qhp = q[:, hp * LANES:(hp + 1) * LANES].astype(BF16)
            s_t = _nt(sub_ref[hp], qhp)
            v, i, _ = _top_rows(s_t, PEER_TOPK)
            tops.append((v, i))
        (v1, i1), (v2, i2) = tops
        cand = jnp.concatenate([v1[a:a + 1] + v2 for a in range(PEER_TOPK)], axis=0)
        cidx = jnp.concatenate([i1[a:a + 1] * float(PEER_NKEYS) + i2 for a in range(PEER_TOPK)], axis=0)
        sc, _, e = _top_rows(cand, PEER_TOPK, payload=cidx)
        ex = jnp.exp(sc - sc[0:1])
        g = ex / jnp.sum(ex, axis=0, keepdims=True)
        eidx_ref[h * PEER_TOPK:(h + 1) * PEER_TOPK, :] = e.astype(I32)
        gw_ref[h * PEER_TOPK:(h + 1) * PEER_TOPK, :] = g


def _mix(o_a, o_b, gates, x, g1, sc2, sh2, n2g, wa, wb, wo, wq, sub, tm):
    B, T, D = x.shape
    blk = lambda n: pl.BlockSpec((1, tm, n), lambda b, i: (b, i, 0))
    cst = lambda a: pl.BlockSpec(a.shape, lambda b, i: (0,) * a.ndim)
    mod = pl.BlockSpec((1, 1, D), lambda b, i: (b, 0, 0))
    nblk = T // tm
    tblk = pl.BlockSpec((PEER_PAIRS, tm), lambda b, i: (0, b * nblk + i))
    return pl.pallas_call(
        _mix_kernel,
        grid=(B, nblk),
        in_specs=[blk(512), blk(512), blk(2 * D), blk(D), mod, mod, mod, cst(n2g), cst(wa), cst(wb),
                  cst(wo), cst(wq), cst(sub)],
        out_specs=[blk(D), blk(D), tblk, tblk],
        out_shape=[jax.ShapeDtypeStruct((B, T, D), F32), jax.ShapeDtypeStruct((B, T, D), F32),
                   jax.ShapeDtypeStruct((PEER_PAIRS, B * T), I32),
                   jax.ShapeDtypeStruct((PEER_PAIRS, B * T), F32)],
        compiler_params=pltpu.CompilerParams(dimension_semantics=("parallel", "parallel"),
                                             vmem_limit_bytes=VMEM_LIMIT),
        name="mix_route",
    )(o_a, o_b, gates, x, g1, sc2, sh2, n2g, wa, wb, wo, wq, sub)


SLAB = 4
ROW_TILES = 8


def _pack_table(t):
    n, d = t.shape
    tb = t.astype(BF16).reshape(n, SLAB, 2, LANES).transpose(0, 1, 3, 2)
    return lax.bitcast_convert_type(tb, I32).reshape(n * SLAB, LANES)


def _load_table(pid, tab_hbm, tab_vmem, sem):
    @pl.when(pid == 0)
    def _():
        cp = pltpu.make_async_copy(tab_hbm, tab_vmem, sem)
        cp.start()
        cp.wait()


def _gather_rows(i, idx_ref, tab_ref, w32_ref):
    for k in range(PEER_PAIRS):
        r = pl.multiple_of(idx_ref[i, k], SLAB)
        w32_ref[k * SLAB:(k + 1) * SLAB, :] = tab_ref[pl.ds(r, SLAB), :]


def _peer_u_kernel(idx_ref, h2_ref, g_ref, diag_ref, gsum_ref, tab_hbm, w_ref, tab_ref, w32_ref, z_ref, sem,
                   *, tb):
    _load_table(pl.program_id(0), tab_hbm, tab_ref, sem)
    diag = diag_ref[...]

    def tok(i, carry):
        _gather_rows(i, idx_ref, tab_ref, w32_ref)
        wb = pltpu.bitcast(w32_ref[...], BF16)
        hi, lo = _split_bf16(h2_ref[i])
        o = _nt(jnp.concatenate([hi, lo], axis=0), wb)
        z_ref[pl.ds(pl.multiple_of(i * ROW_TILES, ROW_TILES), ROW_TILES), :] = (o[:ROW_TILES] + o[ROW_TILES:]) * diag
        return carry

    lax.fori_loop(0, tb, tok, 0)
    a8 = _dot2(z_ref[...], gsum_ref[...])
    a = jnp.sum(a8.reshape(tb, ROW_TILES, PEER_PAIRS), axis=1)
    gelu = 0.5 * a * (1.0 + lax.erf(a * (2.0 ** -0.5)))
    w_ref[...] = g_ref[...] * gelu


def _peer_v_kernel(idx_ref, w_ref, x1_ref, g2_ref, diag_ref, gexp_ref, tab_hbm, o_ref, tab_ref, w32_ref,
                   wrep_ref, sem, *, tb):
    _load_table(pl.program_id(0), tab_hbm, tab_ref, sem)
    diag = diag_ref[...]
    wrep_ref[...] = _dot2(w_ref[...], gexp_ref[...])
    g2 = g2_ref[0]

    def tok(i, carry):
        _gather_rows(i, idx_ref, tab_ref, w32_ref)
        wb = pltpu.bitcast(w32_ref[...], BF16)
        a = wrep_ref[pl.ds(i, 1), :] * diag
        hi, lo = _split_bf16(a)
        o = jnp.dot(jnp.concatenate([hi, lo], axis=0), wb, preferred_element_type=F32)
        o_ref[i] = x1_ref[i] + g2 * (o[:ROW_TILES] + o[ROW_TILES:])
        return carry

    lax.fori_loop(0, tb, tok, 0)


def _peer_consts():
    c = np.arange(PEER_PAIRS * ROW_TILES)
    diag = (c[None, :] % ROW_TILES == np.arange(ROW_TILES)[:, None]).astype(np.float32)
    gsum = (c[:, None] // ROW_TILES == np.arange(PEER_PAIRS)[None, :]).astype(np.float32)
    return jnp.asarray(diag), jnp.asarray(gsum, BF16), jnp.asarray(gsum.T, BF16)


def _peer_u(idx4, h2r, g, utab, tb):
    N = h2r.shape[0]
    diag, gsum, _ = _peer_consts()
    cst = lambda a: pl.BlockSpec(a.shape, lambda i: (0,) * a.ndim)
    return pl.pallas_call(
        functools.partial(_peer_u_kernel, tb=tb),
        grid=(N // tb,),
        in_specs=[pl.BlockSpec((tb, PEER_PAIRS), lambda i: (i, 0), memory_space=pltpu.SMEM),
                  pl.BlockSpec((tb, ROW_TILES, LANES), lambda i: (i, 0, 0)),
                  pl.BlockSpec((tb, PEER_PAIRS), lambda i: (i, 0)),
                  cst(diag), cst(gsum),
                  pl.BlockSpec(memory_space=pl.ANY)],
        out_specs=pl.BlockSpec((tb, PEER_PAIRS), lambda i: (i, 0)),
        out_shape=jax.ShapeDtypeStruct((N, PEER_PAIRS), F32),
        scratch_shapes=[pltpu.VMEM(utab.shape, I32),
                        pltpu.VMEM((PEER_PAIRS * SLAB, LANES), I32),
                        pltpu.VMEM((tb * ROW_TILES, PEER_PAIRS * ROW_TILES), F32),
                        pltpu.SemaphoreType.DMA(())],
        compiler_params=pltpu.CompilerParams(dimension_semantics=("arbitrary",),
                                             vmem_limit_bytes=VMEM_LIMIT),
        name="peer_u",
    )(idx4, h2r, g, diag, gsum, utab)


def _peer_v(idx4, w, x1r, g2r, vtab, tb, blocks_per_batch):
    N = x1r.shape[0]
    diag, _, gexp = _peer_consts()
    cst = lambda a: pl.BlockSpec(a.shape, lambda i: (0,) * a.ndim)
    return pl.pallas_call(
        functools.partial(_peer_v_kernel, tb=tb),
        grid=(N // tb,),
        in_specs=[pl.BlockSpec((tb, PEER_PAIRS), lambda i: (i, 0), memory_space=pltpu.SMEM),
                  pl.BlockSpec((tb, PEER_PAIRS), lambda i: (i, 0)),
                  pl.BlockSpec((tb, ROW_TILES, LANES), lambda i: (i, 0, 0)),
                  pl.BlockSpec((1, ROW_TILES, LANES), lambda i: (i // blocks_per_batch, 0, 0)),
                  cst(diag), cst(gexp),
                  pl.BlockSpec(memory_space=pl.ANY)],
        out_specs=pl.BlockSpec((tb, ROW_TILES, LANES), lambda i: (i, 0, 0)),
        out_shape=jax.ShapeDtypeStruct((N, ROW_TILES, LANES), F32),
        scratch_shapes=[pltpu.VMEM(vtab.shape, I32),
                        pltpu.VMEM((PEER_PAIRS * SLAB, LANES), I32),
                        pltpu.VMEM((tb, PEER_PAIRS * ROW_TILES), F32),
                        pltpu.SemaphoreType.DMA(())],
        compiler_params=pltpu.CompilerParams(dimension_semantics=("arbitrary",),
                                             vmem_limit_bytes=VMEM_LIMIT),
        name="peer_v",
    )(idx4, w, x1r, g2r, diag, gexp, vtab)


def _permute_w_in(w):
    o = np.cumsum([0, 512, 512, 512, 512, 64, 64, 512, 64, 8, 2048])
    dq, dk, dv, sq, sk, sv, iq, ik, iw, gate = [w[:, o[i]:o[i + 1]] for i in range(10)]
    pad = jnp.zeros((w.shape[0], LANES - IDX_HEADS), w.dtype)
    return jnp.concatenate([dq, dk, dv, sq, iq, sk, sk, ik, ik, sv, sv, iw, pad, gate], axis=1)


def _rope_consts(diff_q_g, diff_k_g, dsa_q_g, dsa_k_g):
    inv = ROPE_THETA ** (-(jnp.arange(0, HEAD_DIM, 2, dtype=F32) / HEAD_DIM))
    inv128 = jnp.tile(inv, LANES // HALF).reshape(1, LANES)
    sgn = np.where(np.arange(LANES) % HEAD_DIM < HALF, -1.0, 1.0).astype(np.float32).reshape(1, LANES)
    grp = np.arange(LANES) // HEAD_DIM
    gmat = jnp.asarray((grp[:, None] == grp[None, :]).astype(np.float32) / HEAD_DIM, BF16)
    t8 = lambda g: jnp.tile(g, 8).reshape(1, 512)
    return (inv128, jnp.asarray(sgn), t8(diff_q_g), t8(diff_k_g), t8(dsa_q_g),
            jnp.tile(dsa_k_g, 2).reshape(1, LANES), gmat)


def kernel(x, c, positions, w_ada, b_ada, norm1_g, w_in, diff_q_g, diff_k_g, diff_lam_q1, diff_lam_k1, diff_lam_q2, diff_lam_k2, diff_out_g, dsa_q_g, dsa_k_g, w_branch_a, w_branch_b, w_out, norm2_g, peer_w_q, peer_sub_keys, peer_u, peer_v):
    B, T, D = x.shape
    N = B * T
    assert w_ada.shape[0] == 1, "single-layer block"
    tm = min(256, T)
    tq_diff = min(256, T)
    tq_dsa = min(128, T)
    tk = min(256, T)
    tb = 128

    mod = _ada(c, w_ada[0], b_ada[0])
    sh1, sc1, g1, sh2, sc2, g2 = [m.reshape(B, 1, D) for m in jnp.split(mod, 6, axis=-1)]

    consts = _rope_consts(diff_q_g[0], diff_k_g[0], dsa_q_g[0], dsa_k_g[0])
    w_perm = _permute_w_in(w_in[0]).astype(BF16)
    dq, dk, dv, sq, iq, skk, ikk, svv, iw, gates = _inproj(
        x, sc1, sh1, norm1_g, positions.reshape(B, T, 1), consts, w_perm, tm)

    lams = [v.reshape(1, HEAD_DIM) for v in (diff_lam_q1[0], diff_lam_k1[0], diff_lam_q2[0], diff_lam_k2[0])]
    o_a = _diff_attention(dq, dk, dv, lams, diff_out_g.reshape(1, 2 * HEAD_DIM), tq_diff, tk)
    o_b = _dsa_attention(sq, iq, iw, skk, ikk, svv, tq_dsa, tk)

    sub = peer_sub_keys[0].reshape(2 * PEER_HEADS, PEER_NKEYS, LANES).astype(BF16)
    x1, h2, eidx_t, gw_t = _mix(o_a, o_b, gates, x, g1, sc2, sh2, norm2_g,
                                w_branch_a[0].astype(BF16), w_branch_b[0].astype(BF16),
                                w_out[0].astype(BF16), peer_w_q[0].astype(BF16), sub, tm)

    idx4 = eidx_t.T * SLAB
    gw = gw_t.T
    w = _peer_u(idx4, h2.reshape(N, ROW_TILES, LANES), gw, _pack_table(peer_u[0]), tb)
    out = _peer_v(idx4, w, x1.reshape(N, ROW_TILES, LANES), g2.reshape(B, ROW_TILES, LANES),
                  _pack_table(peer_v[0]), tb, T // tb)
    return out.reshape(B, T, D)
```

```python
import functools
import math

import numpy as np
import jax
import jax.numpy as jnp
from jax import lax
from jax.experimental import pallas as pl
from jax.experimental.pallas import tpu as pltpu

F32 = jnp.float32
BF16 = jnp.bfloat16
I32 = jnp.int32

LANES = 128
HEAD_DIM = 64
HALF = HEAD_DIM // 2
CHUNK = 64
ROPE_THETA = 10000.0
EPS = 1e-6
DIFF_HEADS = 4
DSA_HEADS = 8
IDX_HEADS = 8
DSA_TOPK_MAX = 256
PEER_HEADS = 8
PEER_NKEYS = 128
PEER_TOPK = 16
PEER_PAIRS = PEER_HEADS * PEER_TOPK
LAMBDA_INIT = 0.8 - 0.6 * math.exp(-0.3 * 0)
NEG = -1e30
VMEM_LIMIT = 56 * 1024 * 1024

C_DQ, C_DK, C_DV, C_SQ, C_IQ = 0, 512, 1024, 1536, 2048
C_SK, C_IK, C_SV, C_IW, C_GATE = 2560, 2688, 2816, 2944, 3072
N_COLS = C_GATE + 2048


def _nt(a, b):
    return lax.dot_general(a, b, (((1,), (1,)), ((), ())), preferred_element_type=F32)


def _split_bf16(x):
    hi = x.astype(BF16)
    lo = (x - hi.astype(F32)).astype(BF16)
    return hi, lo


def _dot2(x, w):
    hi, lo = _split_bf16(x)
    return (jnp.dot(hi, w, preferred_element_type=F32)
            + jnp.dot(lo, w, preferred_element_type=F32))


def _ada_kernel(c_ref, w_ref, b_ref, o_ref):
    c = c_ref[...]
    s = c * jax.nn.sigmoid(c)
    o_ref[...] = jnp.dot(s, w_ref[...], preferred_element_type=F32,
                         precision=lax.Precision.HIGHEST) + b_ref[...]


def _ada(c, w, b):
    B, D = c.shape
    n = w.shape[1]
    bn = 1024
    return pl.pallas_call(
        _ada_kernel,
        grid=(n // bn,),
        in_specs=[pl.BlockSpec((B, D), lambda j: (0, 0)),
                  pl.BlockSpec((D, bn), lambda j: (0, j)),
                  pl.BlockSpec((1, bn), lambda j: (0, j))],
        out_specs=pl.BlockSpec((B, bn), lambda j: (0, j)),
        out_shape=jax.ShapeDtypeStruct((B, n), F32),
        compiler_params=pltpu.CompilerParams(dimension_semantics=("parallel",),
                                             vmem_limit_bytes=VMEM_LIMIT),
        name="ada",
    )(c, w, b.reshape(1, n))


def _rope128(y, cos, sin_signed, first_half):
    partner = jnp.where(first_half, pltpu.roll(y, LANES - HALF, 1), pltpu.roll(y, HALF, 1))
    return y * cos + partner * sin_signed


def _group_ms(y, gmat):
    return _dot2(y * y, gmat)


def _inproj_kernel(x_ref, sc_ref, sh_ref, n1_ref, pos_ref, inv_ref, sgn_ref, gq_ref, gk_ref,
                   gsq_ref, gsk_ref, gmat_ref, w_ref,
                   dq_ref, dk_ref, dv_ref, sq_ref, iq_ref, sk_ref, ik_ref, sv_ref, iw_ref, gate_ref):
    x = x_ref[0]
    ms = jnp.mean(x * x, axis=-1, keepdims=True)
    h = x * lax.rsqrt(ms + EPS) * n1_ref[...] * (1.0 + sc_ref[0]) + sh_ref[0]
    hb = h.astype(BF16)
    tm = x.shape[0]

    ang = pos_ref[0].astype(F32) * inv_ref[...]
    cos = jnp.cos(ang)
    sin = jnp.sin(ang) * sgn_ref[...]
    lane = lax.broadcasted_iota(I32, (tm, LANES), 1)
    first_half = (lane % HEAD_DIM) < HALF
    gmat = gmat_ref[...]

    def proj(c0, n):
        return jnp.dot(hb, w_ref[:, c0:c0 + n], preferred_element_type=F32)

    def normed_rope(c0, nblk, gain_ref, out_ref, scale):
        for i in range(nblk):
            y = proj(c0 + i * LANES, LANES)
            if gain_ref is not None:
                y = y * lax.rsqrt(_group_ms(y, gmat) + EPS) * gain_ref[:, i * LANES:(i + 1) * LANES]
            y = _rope128(y, cos, sin, first_half)
            if scale != 1.0:
                y = y * scale
            out_ref[0, :, i * LANES:(i + 1) * LANES] = y.astype(out_ref.dtype)

    qscale = HEAD_DIM ** -0.5
    normed_rope(C_DQ, 4, gq_ref, dq_ref, qscale)
    normed_rope(C_DK, 4, gk_ref, dk_ref, 1.0)
    dv_ref[0] = proj(C_DV, 512).astype(dv_ref.dtype)
    normed_rope(C_SQ, 4, gsq_ref, sq_ref, qscale)
    normed_rope(C_IQ, 4, None, iq_ref, 1.0)
    normed_rope(C_SK, 1, gsk_ref, sk_ref, 1.0)
    normed_rope(C_IK, 1, None, ik_ref, 1.0)
    sv_ref[0] = proj(C_SV, LANES).astype(sv_ref.dtype)
    iw_ref[0] = proj(C_IW, LANES)
    for i in range(4):
        g = proj(C_GATE + i * 512, 512)
        gate_ref[0, :, i * 512:(i + 1) * 512] = jax.nn.sigmoid(g).astype(gate_ref.dtype)


def _inproj(x, sc1, sh1, n1g, pos, consts, w_perm, tm):
    B, T, D = x.shape
    inv128, sgn128, gq, gk, gsq, gsk, gmat = consts
    tok = lambda n, dt: jax.ShapeDtypeStruct((B, T, n), dt)
    blk = lambda n: pl.BlockSpec((1, tm, n), lambda b, i: (b, i, 0))
    cst = lambda a: pl.BlockSpec(a.shape, lambda b, i: (0,) * a.ndim)
    mod = pl.BlockSpec((1, 1, D), lambda b, i: (b, 0, 0))
    return pl.pallas_call(
        _inproj_kernel,
        grid=(B, T // tm),
        in_specs=[blk(D), mod, mod, cst(n1g), blk(1), cst(inv128), cst(sgn128), cst(gq), cst(gk),
                  cst(gsq), cst(gsk), cst(gmat), cst(w_perm)],
        out_specs=[blk(512), blk(512), blk(512), blk(512), blk(512), blk(128), blk(128), blk(128),
                   blk(128), blk(2048)],
        out_shape=[tok(512, BF16), tok(512, BF16), tok(512, BF16), tok(512, BF16), tok(512, BF16),
                   tok(128, BF16), tok(128, BF16), tok(128, BF16), tok(128, F32), tok(2048, BF16)],
        compiler_params=pltpu.CompilerParams(dimension_semantics=("parallel", "parallel"),
                                             vmem_limit_bytes=VMEM_LIMIT),
        name="inproj",
    )(x, sc1, sh1, n1g, pos, inv128, sgn128, gq, gk, gsq, gsk, gmat, w_perm)


def _diff_kernel(q_ref, k_ref, v_ref, lq1_ref, lk1_ref, lq2_ref, lk2_ref, og_ref, o_ref, *, tq, tk):
    qi = pl.program_id(2)
    q = q_ref[0]
    lane = lax.broadcasted_iota(I32, (tq, LANES), 1)
    zero = jnp.zeros_like(q)
    q2 = jnp.concatenate([jnp.where(lane < HEAD_DIM, q, zero),
                          jnp.where(lane >= HEAD_DIM, q, zero)], axis=0)
    row = lax.broadcasted_iota(I32, (2 * tq, tk), 0)
    col = lax.broadcasted_iota(I32, (2 * tq, tk), 1)
    q_chunk = (qi * tq + jnp.where(row >= tq, row - tq, row)) // CHUNK

    def body(j, carry):
        m, l, acc = carry
        start = pl.multiple_of(j * tk, tk)
        kt = k_ref[0, pl.ds(start, tk), :]
        vt = v_ref[0, pl.ds(start, tk), :]
        s = _nt(q2, kt)
        ok = (j * tk + col) // CHUNK <= q_chunk
        s = jnp.where(ok, s, NEG)
        m_new = jnp.maximum(m, jnp.max(s, axis=-1, keepdims=True))
        alpha = jnp.exp(m - m_new)
        p = jnp.where(ok, jnp.exp(s - m_new), 0.0)
        l = alpha * l + jnp.sum(p, axis=-1, keepdims=True)
        acc = alpha * acc + jnp.dot(p.astype(BF16), vt, preferred_element_type=F32)
        return m_new, l, acc

    init = (jnp.full((2 * tq, 1), NEG, F32), jnp.zeros((2 * tq, 1), F32),
            jnp.zeros((2 * tq, LANES), F32))
    n_kv = (qi * tq + tq + tk - 1) // tk
    m, l, acc = lax.fori_loop(0, n_kv, body, init)
    o = acc / l
    lam = (jnp.exp(jnp.sum(lq1_ref[...] * lk1_ref[...], axis=-1, keepdims=True))
           - jnp.exp(jnp.sum(lq2_ref[...] * lk2_ref[...], axis=-1, keepdims=True)) + LAMBDA_INIT)
    o = o[:tq] - lam * o[tq:]
    ms = jnp.mean(o * o, axis=-1, keepdims=True)
    o = o * lax.rsqrt(ms + EPS) * og_ref[...] * (1.0 - LAMBDA_INIT)
    o_ref[0] = o.astype(o_ref.dtype)


def _diff_attention(dq, dk, dv, lams, out_g, tq, tk):
    B, T, _ = dq.shape
    cst = lambda a: pl.BlockSpec(a.shape, lambda b, h, i: (0,) * a.ndim)
    return pl.pallas_call(
        functools.partial(_diff_kernel, tq=tq, tk=tk),
        grid=(B, DIFF_HEADS, T // tq),
        in_specs=[pl.BlockSpec((1, tq, LANES), lambda b, h, i: (b, i, h)),
                  pl.BlockSpec((1, T, LANES), lambda b, h, i: (b, 0, h)),
                  pl.BlockSpec((1, T, LANES), lambda b, h, i: (b, 0, h)),
                  cst(lams[0]), cst(lams[1]), cst(lams[2]), cst(lams[3]), cst(out_g)],
        out_specs=pl.BlockSpec((1, tq, LANES), lambda b, h, i: (b, i, h)),
        out_shape=jax.ShapeDtypeStruct((B, T, DIFF_HEADS * LANES), BF16),
        compiler_params=pltpu.CompilerParams(
            dimension_semantics=("parallel", "parallel", "parallel"), vmem_limit_bytes=VMEM_LIMIT),
        name="diff_attention",
    )(dq, dk, dv, *lams, out_g)


def _dsa_kernel(sq_ref, iq_ref, iw_ref, sk_ref, ik_ref, sv_ref, o_ref, key_ref, *, tq, tk, topk, t_bits):
    qi = pl.program_id(1)
    n_kv = (qi * tq + tq + tk - 1) // tk
    lane = lax.broadcasted_iota(I32, (tq, LANES), 1)
    lo_half = lane < HEAD_DIM
    row = lax.broadcasted_iota(I32, (tq, tk), 0)
    col = lax.broadcasted_iota(I32, (tq, tk), 1)
    q_chunk = (qi * tq + row) // CHUNK
    idx_scale = (IDX_HEADS ** -0.5) * (HEAD_DIM ** -0.5)
    key_neg_inf = jnp.int32(-2139095041)

    def head_slabs(ref):
        out = []
        for p in range(4):
            slab = ref[0, :, p * LANES:(p + 1) * LANES]
            zero = jnp.zeros_like(slab)
            out.append(jnp.where(lo_half, slab, zero))
            out.append(jnp.where(lo_half, zero, slab))
        return out

    iq_heads = head_slabs(iq_ref)
    iw = iw_ref[0]

    def score_body(j, carry):
        start = pl.multiple_of(j * tk, tk)
        ikt = ik_ref[0, pl.ds(start, tk), :]
        score = jnp.zeros((tq, tk), F32)
        for h in range(IDX_HEADS):
            d = _nt(iq_heads[h], ikt)
            score = score + jnp.maximum(d, 0.0) * iw[:, h:h + 1]
        score = score * idx_scale + 0.0
        ok = (j * tk + col) // CHUNK <= q_chunk
        score = jnp.where(ok, score, -jnp.inf)
        bits = lax.bitcast_convert_type(score, I32)
        key_ref[j] = bits ^ ((bits >> 31) & jnp.int32(0x7FFFFFFF))
        return carry

    lax.fori_loop(0, n_kv, score_body, 0)

    def count(pred_fn):
        def body(j, c):
            kj = key_ref[j]
            hit = jnp.where(pred_fn(kj, j * tk + col), 1.0, 0.0)
            for b in range(tk // LANES):
                c = c + hit[:, b * LANES:(b + 1) * LANES]
            return c
        c = lax.fori_loop(0, n_kv, body, jnp.zeros((tq, LANES), F32))
        return jnp.sum(c, axis=-1, keepdims=True)

    def bit_body(it, t):
        cand = t + jnp.left_shift(jnp.int32(1), 31 - it)
        c = count(lambda kj, _: kj >= cand)
        return jnp.where(c >= float(topk), cand, t)

    thr = lax.fori_loop(0, 32, bit_body, jnp.full((tq, 1), jnp.iinfo(jnp.int32).min, I32))
    need = float(topk) - count(lambda kj, _: kj > thr)

    def tie_body(it, jmax):
        cand = jmax + jnp.left_shift(jnp.int32(1), t_bits - 1 - it)
        c = count(lambda kj, kidx: (kj == thr) & (kidx < cand))
        return jnp.where(c < need, cand, jmax)

    jmax = lax.fori_loop(0, t_bits, tie_body, jnp.zeros((tq, 1), I32))

    q_all = jnp.concatenate(head_slabs(sq_ref), axis=0)

    def attn_body(j, carry):
        m, l, acc = carry
        start = pl.multiple_of(j * tk, tk)
        kt = sk_ref[0, pl.ds(start, tk), :]
        vt = sv_ref[0, pl.ds(start, tk), :]
        kj = key_ref[j]
        kidx = j * tk + col
        sel = (kj > thr) | ((kj == thr) & (kidx <= jmax))
        sel = sel & (kj != key_neg_inf)
        sel1 = jnp.where(sel, 1.0, 0.0)
        sel8 = jnp.concatenate([sel1] * DSA_HEADS, axis=0)
        s = jnp.where(sel8 > 0.5, _nt(q_all, kt), NEG)
        m_new = jnp.maximum(m, jnp.max(s, axis=-1, keepdims=True))
        alpha = jnp.exp(m - m_new)
        p = jnp.exp(s - m_new) * sel8
        l = alpha * l + jnp.sum(p, axis=-1, keepdims=True)
        acc = alpha * acc + jnp.dot(p.astype(BF16), vt, preferred_element_type=F32)
        return m_new, l, acc

    rows = DSA_HEADS * tq
    init = (jnp.full((rows, 1), NEG, F32), jnp.zeros((rows, 1), F32), jnp.zeros((rows, LANES), F32))
    _, l, acc = lax.fori_loop(0, n_kv, attn_body, init)
    o = acc / l
    for p in range(4):
        even = o[(2 * p) * tq:(2 * p + 1) * tq]
        odd = o[(2 * p + 1) * tq:(2 * p + 2) * tq]
        o_ref[0, :, p * LANES:(p + 1) * LANES] = jnp.where(lo_half, even, odd).astype(o_ref.dtype)


def _dsa_attention(sq, iq, iw, skk, ikk, svv, tq, tk):
    B, T, _ = sq.shape
    topk = min(DSA_TOPK_MAX, T // 4)
    t_bits = max(1, (T - 1).bit_length())
    qblk = lambda n: pl.BlockSpec((1, tq, n), lambda b, i: (b, i, 0))
    kblk = pl.BlockSpec((1, T, LANES), lambda b, i: (b, 0, 0))
    return pl.pallas_call(
        functools.partial(_dsa_kernel, tq=tq, tk=tk, topk=topk, t_bits=t_bits),
        grid=(B, T // tq),
        in_specs=[qblk(512), qblk(512), qblk(LANES), kblk, kblk, kblk],
        out_specs=qblk(512),
        out_shape=jax.ShapeDtypeStruct((B, T, DSA_HEADS * HEAD_DIM), BF16),
        scratch_shapes=[pltpu.VMEM((T // tk, tq, tk), I32)],
        compiler_params=pltpu.CompilerParams(dimension_semantics=("parallel", "parallel"),
                                             vmem_limit_bytes=VMEM_LIMIT),
        name="dsa_attention",
    )(sq, iq, iw, skk, ikk, svv)


def _top_rows(s, k, payload=None):
    R, n = s.shape
    rid = lax.broadcasted_iota(I32, (R, n), 0).astype(F32)
    vals, ids, pay = [], [], []
    for _ in range(k):
        m = jnp.max(s, axis=0, keepdims=True)
        am = jnp.min(jnp.where(s == m, rid, float(R)), axis=0, keepdims=True)
        hit = rid == am
        vals.append(m)
        ids.append(am)
        if payload is not None:
            pay.append(jnp.sum(jnp.where(hit, payload, 0.0), axis=0, keepdims=True))
        s = jnp.where(hit, -jnp.inf, s)
    cat = lambda xs: jnp.concatenate(xs, axis=0)
    return cat(vals), cat(ids), (cat(pay) if payload is not None else None)


def _mix_kernel(oa_ref, ob_ref, gate_ref, x_ref, g1_ref, sc_ref, sh_ref, n2_ref, wa_ref, wb_ref, wo_ref,
                wq_ref, sub_ref, x1_ref, h2_ref, eidx_ref, gw_ref):
    ya = jnp.dot(oa_ref[0], wa_ref[...], preferred_element_type=F32)
    yb = jnp.dot(ob_ref[0], wb_ref[...], preferred_element_type=F32)
    D = ya.shape[1]
    merged = gate_ref[0, :, :D].astype(F32) * ya + gate_ref[0, :, D:].astype(F32) * yb
    x1 = x_ref[0] + g1_ref[0] * jnp.dot(merged.astype(BF16), wo_ref[...], preferred_element_type=F32)
    x1_ref[0] = x1
    ms = jnp.mean(x1 * x1, axis=-1, keepdims=True)
    h2 = x1 * lax.rsqrt(ms + EPS) * n2_ref[...] * (1.0 + sc_ref[0]) + sh_ref[0]
    h2_ref[0] = h2
    q = jnp.dot(h2.astype(BF16), wq_ref[...], preferred_element_type=F32)

    for h in range(PEER_HEADS):
        tops = []
        for p in range(2):
            hp = 2 * h + p
            qhp = q[:, hp * LANES:(hp + 1) * LANES].astype(BF16)
            s_t = _nt(sub_ref[hp], qhp)
            v, i, _ = _top_rows(s_t, PEER_TOPK)
            tops.append((v, i))
        (v1, i1), (v2, i2) = tops
        cand = jnp.concatenate([v1[a:a + 1] + v2 for a in range(PEER_TOPK)], axis=0)
        cidx = jnp.concatenate([i1[a:a + 1] * float(PEER_NKEYS) + i2 for a in range(PEER_TOPK)], axis=0)
        sc, _, e = _top_rows(cand, PEER_TOPK, payload=cidx)
        ex = jnp.exp(sc - sc[0:1])
        g = ex / jnp.sum(ex, axis=0, keepdims=True)
        eidx_ref[h * PEER_TOPK:(h + 1) * PEER_TOPK, :] = e.astype(I32)
        gw_ref[h * PEER_TOPK:(h + 1) * PEER_TOPK, :] = g


def _mix(o_a, o_b, gates, x, g1, sc2, sh2, n2g, wa, wb, wo, wq, sub, tm):
    B, T, D = x.shape
    blk = lambda n: pl.BlockSpec((1, tm, n), lambda b, i: (b, i, 0))
    cst = lambda a: pl.BlockSpec(a.shape, lambda b, i: (0,) * a.ndim)
    mod = pl.BlockSpec((1, 1, D), lambda b, i: (b, 0, 0))
    nblk = T // tm
    tblk = pl.BlockSpec((PEER_PAIRS, tm), lambda b, i: (0, b * nblk + i))
    return pl.pallas_call(
        _mix_kernel,
        grid=(B, nblk),
        in_specs=[blk(512), blk(512), blk(2 * D), blk(D), mod, mod, mod, cst(n2g), cst(wa), cst(wb),
                  cst(wo), cst(wq), cst(sub)],
        out_specs=[blk(D), blk(D), tblk, tblk],
        out_shape=[jax.ShapeDtypeStruct((B, T, D), F32), jax.ShapeDtypeStruct((B, T, D), F32),
                   jax.ShapeDtypeStruct((PEER_PAIRS, B * T), I32),
                   jax.ShapeDtypeStruct((PEER_PAIRS, B * T), F32)],
        compiler_params=pltpu.CompilerParams(dimension_semantics=("parallel", "parallel"),
                                             vmem_limit_bytes=VMEM_LIMIT),
        name="mix_route",
    )(o_a, o_b, gates, x, g1, sc2, sh2, n2g, wa, wb, wo, wq, sub)


SLAB = 4
ROW_TILES = 8
GATHER_LEAD = 1
N_GATHER_BUFS = 2 * GATHER_LEAD
TOKENS_PER_TRIP = 8


def _gather_scratch():
    return [pltpu.VMEM((PEER_PAIRS * SLAB, LANES), I32) for _ in range(N_GATHER_BUFS)]


def _pack_table(t):
    n, d = t.shape
    tb = t.astype(BF16).reshape(n, SLAB, 2, LANES).transpose(0, 1, 3, 2)
    return lax.bitcast_convert_type(tb, I32).reshape(n * SLAB, LANES)


def _resident_table_spec(tab):
    return pl.BlockSpec(tab.shape, lambda i: (0, 0), pipeline_mode=pl.Buffered(1))


def _gather_rows(i, idx_ref, tab_ref, w32_ref):
    for k in range(PEER_PAIRS):
        r = pl.multiple_of(idx_ref[i, k], SLAB)
        w32_ref[k * SLAB:(k + 1) * SLAB, :] = tab_ref[pl.ds(r, SLAB), :]


def _skewed_token_loop(tb, gather, compute, bufs):
    n = len(bufs)
    for i in range(GATHER_LEAD):
        gather(i, bufs[i])

    def group(j, carry):
        i0 = TOKENS_PER_TRIP * j
        for u in range(TOKENS_PER_TRIP):
            gather(jnp.minimum(i0 + u + GATHER_LEAD, tb - 1), bufs[(u + GATHER_LEAD) % n])
            compute(i0 + u, bufs[u % n])
        return carry

    lax.fori_loop(0, tb // TOKENS_PER_TRIP, group, 0)


def _peer_u_kernel(idx_ref, h2_ref, g_ref, diag_ref, gsum_ref, tab_ref, w_ref, z_ref, *bufs, tb):
    diag = diag_ref[...]

    def compute(i, w32_ref):
        wb = pltpu.bitcast(w32_ref[...], BF16)
        hi, lo = _split_bf16(h2_ref[i])
        o = _nt(jnp.concatenate([hi, lo], axis=0), wb)
        z_ref[pl.ds(pl.multiple_of(i * ROW_TILES, ROW_TILES), ROW_TILES), :] = (o[:ROW_TILES] + o[ROW_TILES:]) * diag

    _skewed_token_loop(tb, lambda i, buf: _gather_rows(i, idx_ref, tab_ref, buf), compute, bufs)
    a8 = _dot2(z_ref[...], gsum_ref[...])
    a = jnp.sum(a8.reshape(tb, ROW_TILES, PEER_PAIRS), axis=1)
    gelu = 0.5 * a * (1.0 + lax.erf(a * (2.0 ** -0.5)))
    w_ref[...] = g_ref[...] * gelu


def _peer_v_kernel(idx_ref, w_ref, x1_ref, g2_ref, diag_ref, gexp_ref, tab_ref, o_ref, wrep_ref, *bufs, tb):
    diag = diag_ref[...]
    wrep_ref[...] = _dot2(w_ref[...], gexp_ref[...])
    g2 = g2_ref[0]

    def compute(i, w32_ref):
        wb = pltpu.bitcast(w32_ref[...], BF16)
        a = wrep_ref[pl.ds(i, 1), :] * diag
        hi, lo = _split_bf16(a)
        o = jnp.dot(jnp.concatenate([hi, lo], axis=0), wb, preferred_element_type=F32)
        o_ref[i] = x1_ref[i] + g2 * (o[:ROW_TILES] + o[ROW_TILES:])

    _skewed_token_loop(tb, lambda i, buf: _gather_rows(i, idx_ref, tab_ref, buf), compute, bufs)


def _peer_consts():
    c = np.arange(PEER_PAIRS * ROW_TILES)
    diag = (c[None, :] % ROW_TILES == np.arange(ROW_TILES)[:, None]).astype(np.float32)
    gsum = (c[:, None] // ROW_TILES == np.arange(PEER_PAIRS)[None, :]).astype(np.float32)
    return jnp.asarray(diag), jnp.asarray(gsum, BF16), jnp.asarray(gsum.T, BF16)


def _peer_u(idx4, h2r, g, utab, tb):
    N = h2r.shape[0]
    diag, gsum, _ = _peer_consts()
    cst = lambda a: pl.BlockSpec(a.shape, lambda i: (0,) * a.ndim)
    return pl.pallas_call(
        functools.partial(_peer_u_kernel, tb=tb),
        grid=(N // tb,),
        in_specs=[pl.BlockSpec((tb, PEER_PAIRS), lambda i: (i, 0), memory_space=pltpu.SMEM),
                  pl.BlockSpec((tb, ROW_TILES, LANES), lambda i: (i, 0, 0)),
                  pl.BlockSpec((tb, PEER_PAIRS), lambda i: (i, 0)),
                  cst(diag), cst(gsum), _resident_table_spec(utab)],
        out_specs=pl.BlockSpec((tb, PEER_PAIRS), lambda i: (i, 0)),
        out_shape=jax.ShapeDtypeStruct((N, PEER_PAIRS), F32),
        scratch_shapes=[pltpu.VMEM((tb * ROW_TILES, PEER_PAIRS * ROW_TILES), F32)] + _gather_scratch(),
        compiler_params=pltpu.CompilerParams(dimension_semantics=("arbitrary",),
                                             vmem_limit_bytes=VMEM_LIMIT),
        name="peer_u",
    )(idx4, h2r, g, diag, gsum, utab)


def _peer_v(idx4, w, x1r, g2r, vtab, tb, blocks_per_batch):
    N = x1r.shape[0]
    diag, _, gexp = _peer_consts()
    cst = lambda a: pl.BlockSpec(a.shape, lambda i: (0,) * a.ndim)
    return pl.pallas_call(
        functools.partial(_peer_v_kernel, tb=tb),
        grid=(N // tb,),
        in_specs=[pl.BlockSpec((tb, PEER_PAIRS), lambda i: (i, 0), memory_space=pltpu.SMEM),
                  pl.BlockSpec((tb, PEER_PAIRS), lambda i: (i, 0)),
                  pl.BlockSpec((tb, ROW_TILES, LANES), lambda i: (i, 0, 0)),
                  pl.BlockSpec((1, ROW_TILES, LANES), lambda i: (i // blocks_per_batch, 0, 0)),
                  cst(diag), cst(gexp), _resident_table_spec(vtab)],
        out_specs=pl.BlockSpec((tb, ROW_TILES, LANES), lambda i: (i, 0, 0)),
        out_shape=jax.ShapeDtypeStruct((N, ROW_TILES, LANES), F32),
        scratch_shapes=[pltpu.VMEM((tb, PEER_PAIRS * ROW_TILES), F32)] + _gather_scratch(),
        compiler_params=pltpu.CompilerParams(dimension_semantics=("arbitrary",),
                                             vmem_limit_bytes=VMEM_LIMIT),
        name="peer_v",
    )(idx4, w, x1r, g2r, diag, gexp, vtab)


def _permute_w_in(w):
    o = np.cumsum([0, 512, 512, 512, 512, 64, 64, 512, 64, 8, 2048])
    dq, dk, dv, sq, sk, sv, iq, ik, iw, gate = [w[:, o[i]:o[i + 1]] for i in range(10)]
    pad = jnp.zeros((w.shape[0], LANES - IDX_HEADS), w.dtype)
    return jnp.concatenate([dq, dk, dv, sq, iq, sk, sk, ik, ik, sv, sv, iw, pad, gate], axis=1)


def _rope_consts(diff_q_g, diff_k_g, dsa_q_g, dsa_k_g):
    inv = ROPE_THETA ** (-(jnp.arange(0, HEAD_DIM, 2, dtype=F32) / HEAD_DIM))
    inv128 = jnp.tile(inv, LANES // HALF).reshape(1, LANES)
    sgn = np.where(np.arange(LANES) % HEAD_DIM < HALF, -1.0, 1.0).astype(np.float32).reshape(1, LANES)
    grp = np.arange(LANES) // HEAD_DIM
    gmat = jnp.asarray((grp[:, None] == grp[None, :]).astype(np.float32) / HEAD_DIM, BF16)
    t8 = lambda g: jnp.tile(g, 8).reshape(1, 512)
    return (inv128, jnp.asarray(sgn), t8(diff_q_g), t8(diff_k_g), t8(dsa_q_g),
            jnp.tile(dsa_k_g, 2).reshape(1, LANES), gmat)


def kernel(x, c, positions, w_ada, b_ada, norm1_g, w_in, diff_q_g, diff_k_g, diff_lam_q1, diff_lam_k1, diff_lam_q2, diff_lam_k2, diff_out_g, dsa_q_g, dsa_k_g, w_branch_a, w_branch_b, w_out, norm2_g, peer_w_q, peer_sub_keys, peer_u, peer_v):
    B, T, D = x.shape
    N = B * T
    assert w_ada.shape[0] == 1, "single-layer block"
    tm = min(256, T)
    tq_diff = min(256, T)
    tq_dsa = min(128, T)
    tk = min(256, T)
    tb = 128

    mod = _ada(c, w_ada[0], b_ada[0])
    sh1, sc1, g1, sh2, sc2, g2 = [m.reshape(B, 1, D) for m in jnp.split(mod, 6, axis=-1)]

    consts = _rope_consts(diff_q_g[0], diff_k_g[0], dsa_q_g[0], dsa_k_g[0])
    w_perm = _permute_w_in(w_in[0]).astype(BF16)
    dq, dk, dv, sq, iq, skk, ikk, svv, iw, gates = _inproj(
        x, sc1, sh1, norm1_g, positions.reshape(B, T, 1), consts, w_perm, tm)

    lams = [v.reshape(1, HEAD_DIM) for v in (diff_lam_q1[0], diff_lam_k1[0], diff_lam_q2[0], diff_lam_k2[0])]
    o_a = _diff_attention(dq, dk, dv, lams, diff_out_g.reshape(1, 2 * HEAD_DIM), tq_diff, tk)
    o_b = _dsa_attention(sq, iq, iw, skk, ikk, svv, tq_dsa, tk)

    sub = peer_sub_keys[0].reshape(2 * PEER_HEADS, PEER_NKEYS, LANES).astype(BF16)
    x1, h2, eidx_t, gw_t = _mix(o_a, o_b, gates, x, g1, sc2, sh2, norm2_g,
                                w_branch_a[0].astype(BF16), w_branch_b[0].astype(BF16),
                                w_out[0].astype(BF16), peer_w_q[0].astype(BF16), sub, tm)

    idx4 = eidx_t.T * SLAB
    gw = gw_t.T
    w = _peer_u(idx4, h2.reshape(N, ROW_TILES, LANES), gw, _pack_table(peer_u[0]), tb)
    out = _peer_v(idx4, w, x1.reshape(N, ROW_TILES, LANES), g2.reshape(B, ROW_TILES, LANES),
                  _pack_table(peer_v[0]), tb, T // tb)
    return out.reshape(B, T, D)
```

```python
import functools
import math

import numpy as np
import jax
import jax.numpy as jnp
from jax import lax
from jax.experimental import pallas as pl
from jax.experimental.pallas import tpu as pltpu

F32 = jnp.float32
BF16 = jnp.bfloat16
I32 = jnp.int32

LANES = 128
HEAD_DIM = 64
HALF = HEAD_DIM // 2
CHUNK = 64
ROPE_THETA = 10000.0
EPS = 1e-6
DIFF_HEADS = 4
DSA_HEADS = 8
IDX_HEADS = 8
DSA_TOPK_MAX = 256
PEER_HEADS = 8
PEER_NKEYS = 128
PEER_TOPK = 16
PEER_PAIRS = PEER_HEADS * PEER_TOPK
LAMBDA_INIT = 0.8 - 0.6 * math.exp(-0.3 * 0)
NEG = -1e30
VMEM_LIMIT = 56 * 1024 * 1024

C_DQ, C_DK, C_DV, C_SQ, C_IQ = 0, 512, 1024, 1536, 2048
C_SK, C_IK, C_SV, C_IW, C_GATE = 2560, 2688, 2816, 2944, 3072
N_COLS = C_GATE + 2048


def _nt(a, b):
    return lax.dot_general(a, b, (((1,), (1,)), ((), ())), preferred_element_type=F32)


def _split_bf16(x):
    hi = x.astype(BF16)
    lo = (x - hi.astype(F32)).astype(BF16)
    return hi, lo


def _dot2(x, w):
    hi, lo = _split_bf16(x)
    return (jnp.dot(hi, w, preferred_element_type=F32)
            + jnp.dot(lo, w, preferred_element_type=F32))


def _ada_kernel(c_ref, w_ref, b_ref, o_ref):
    c = c_ref[...]
    s = c * jax.nn.sigmoid(c)
    o_ref[...] = jnp.dot(s, w_ref[...], preferred_element_type=F32,
                         precision=lax.Precision.HIGHEST) + b_ref[...]


def _ada(c, w, b):
    B, D = c.shape
    n = w.shape[1]
    bn = 1024
    return pl.pallas_call(
        _ada_kernel,
        grid=(n // bn,),
        in_specs=[pl.BlockSpec((B, D), lambda j: (0, 0)),
                  pl.BlockSpec((D, bn), lambda j: (0, j)),
                  pl.BlockSpec((1, bn), lambda j: (0, j))],
        out_specs=pl.BlockSpec((B, bn), lambda j: (0, j)),
        out_shape=jax.ShapeDtypeStruct((B, n), F32),
        compiler_params=pltpu.CompilerParams(dimension_semantics=("parallel",),
                                             vmem_limit_bytes=VMEM_LIMIT),
        name="ada",
    )(c, w, b.reshape(1, n))


def _rope128(y, cos, sin_signed, first_half):
    partner = jnp.where(first_half, pltpu.roll(y, LANES - HALF, 1), pltpu.roll(y, HALF, 1))
    return y * cos + partner * sin_signed


def _group_ms(y, gmat):
    return _dot2(y * y, gmat)


def _inproj_kernel(x_ref, sc_ref, sh_ref, n1_ref, pos_ref, inv_ref, sgn_ref, gq_ref, gk_ref,
                   gsq_ref, gsk_ref, gmat_ref, w_ref,
                   dq_ref, dk_ref, dv_ref, sq_ref, iq_ref, sk_ref, ik_ref, sv_ref, iw_ref, gate_ref):
    x = x_ref[0]
    ms = jnp.mean(x * x, axis=-1, keepdims=True)
    h = x * lax.rsqrt(ms + EPS) * n1_ref[...] * (1.0 + sc_ref[0]) + sh_ref[0]
    hb = h.astype(BF16)
    tm = x.shape[0]

    ang = pos_ref[0].astype(F32) * inv_ref[...]
    cos = jnp.cos(ang)
    sin = jnp.sin(ang) * sgn_ref[...]
    lane = lax.broadcasted_iota(I32, (tm, LANES), 1)
    first_half = (lane % HEAD_DIM) < HALF
    gmat = gmat_ref[...]

    def proj(c0, n):
        return jnp.dot(hb, w_ref[:, c0:c0 + n], preferred_element_type=F32)

    def normed_rope(c0, nblk, gain_ref, out_ref, scale):
        for i in range(nblk):
            y = proj(c0 + i * LANES, LANES)
            if gain_ref is not None:
                y = y * lax.rsqrt(_group_ms(y, gmat) + EPS) * gain_ref[:, i * LANES:(i + 1) * LANES]
            y = _rope128(y, cos, sin, first_half)
            if scale != 1.0:
                y = y * scale
            out_ref[0, :, i * LANES:(i + 1) * LANES] = y.astype(out_ref.dtype)

    qscale = HEAD_DIM ** -0.5
    normed_rope(C_DQ, 4, gq_ref, dq_ref, qscale)
    normed_rope(C_DK, 4, gk_ref, dk_ref, 1.0)
    dv_ref[0] = proj(C_DV, 512).astype(dv_ref.dtype)
    normed_rope(C_SQ, 4, gsq_ref, sq_ref, qscale)
    normed_rope(C_IQ, 4, None, iq_ref, 1.0)
    normed_rope(C_SK, 1, gsk_ref, sk_ref, 1.0)
    normed_rope(C_IK, 1, None, ik_ref, 1.0)
    sv_ref[0] = proj(C_SV, LANES).astype(sv_ref.dtype)
    iw_ref[0] = proj(C_IW, LANES)
    for i in range(4):
        g = proj(C_GATE + i * 512, 512)
        gate_ref[0, :, i * 512:(i + 1) * 512] = jax.nn.sigmoid(g).astype(gate_ref.dtype)


def _inproj(x, sc1, sh1, n1g, pos, consts, w_perm, tm):
    B, T, D = x.shape
    inv128, sgn128, gq, gk, gsq, gsk, gmat = consts
    tok = lambda n, dt: jax.ShapeDtypeStruct((B, T, n), dt)
    blk = lambda n: pl.BlockSpec((1, tm, n), lambda b, i: (b, i, 0))
    cst = lambda a: pl.BlockSpec(a.shape, lambda b, i: (0,) * a.ndim)
    mod = pl.BlockSpec((1, 1, D), lambda b, i: (b, 0, 0))
    return pl.pallas_call(
        _inproj_kernel,
        grid=(B, T // tm),
        in_specs=[blk(D), mod, mod, cst(n1g), blk(1), cst(inv128), cst(sgn128), cst(gq), cst(gk),
                  cst(gsq), cst(gsk), cst(gmat), cst(w_perm)],
        out_specs=[blk(512), blk(512), blk(512), blk(512), blk(512), blk(128), blk(128), blk(128),
                   blk(128), blk(2048)],
        out_shape=[tok(512, BF16), tok(512, BF16), tok(512, BF16), tok(512, BF16), tok(512, BF16),
                   tok(128, BF16), tok(128, BF16), tok(128, BF16), tok(128, F32), tok(2048, BF16)],
        compiler_params=pltpu.CompilerParams(dimension_semantics=("parallel", "parallel"),
                                             vmem_limit_bytes=VMEM_LIMIT),
        name="inproj",
    )(x, sc1, sh1, n1g, pos, inv128, sgn128, gq, gk, gsq, gsk, gmat, w_perm)


def _diff_kernel(q_ref, k_ref, vt_ref, lq1_ref, lk1_ref, lq2_ref, lk2_ref, og_ref, o_ref, *, tq):
    qi = pl.program_id(1)
    lane = lax.broadcasted_iota(I32, (tq, LANES), 1)
    q2 = []
    for h in range(DIFF_HEADS):
        q = q_ref[0, :, h * LANES:(h + 1) * LANES]
        zero = jnp.zeros_like(q)
        q2.append(jnp.concatenate([jnp.where(lane < HEAD_DIM, q, zero),
                                   jnp.where(lane >= HEAD_DIM, q, zero)], axis=0))

    def tile(j, carry, on_diagonal):
        out = []
        for h in range(DIFF_HEADS):
            m, l, acc = carry[h]
            s = _nt(k_ref[0, j, :, h * LANES:(h + 1) * LANES], q2[h])
            if on_diagonal:
                krow = lax.broadcasted_iota(I32, (tq, 2 * tq), 0)
                qcol = lax.broadcasted_iota(I32, (tq, 2 * tq), 1)
                qcol = jnp.where(qcol >= tq, qcol - tq, qcol)
                s = jnp.where(krow // CHUNK <= qcol // CHUNK, s, NEG)
            m_new = jnp.maximum(m, jnp.max(s, axis=0, keepdims=True))
            alpha = jnp.exp(m - m_new)
            p = jnp.exp(s - m_new)
            l = alpha * l + jnp.sum(p, axis=0, keepdims=True)
            acc = alpha * acc + jnp.dot(vt_ref[0, h, j], p.astype(BF16), preferred_element_type=F32)
            out.append((m_new, l, acc))
        return tuple(out)

    init = tuple((jnp.full((1, 2 * tq), NEG, F32), jnp.zeros((1, 2 * tq), F32),
                  jnp.zeros((LANES, 2 * tq), F32)) for _ in range(DIFF_HEADS))
    carry = lax.fori_loop(0, qi, lambda j, c: tile(j, c, False), init)
    res = tile(qi, carry, True)
    lam = (jnp.exp(jnp.sum(lq1_ref[...] * lk1_ref[...], axis=-1, keepdims=True))
           - jnp.exp(jnp.sum(lq2_ref[...] * lk2_ref[...], axis=-1, keepdims=True)) + LAMBDA_INIT)
    for h in range(DIFF_HEADS):
        _, l, acc = res[h]
        o = acc * (1.0 / l)
        o = o[:, :tq] - lam * o[:, tq:]
        ms = jnp.mean(o * o, axis=0, keepdims=True)
        o = o * lax.rsqrt(ms + EPS) * og_ref[...] * (1.0 - LAMBDA_INIT)
        o_ref[0, :, h * LANES:(h + 1) * LANES] = o.T.astype(o_ref.dtype)


def _diff_attention(dq, dk, dv, lams, out_g, tq):
    B, T, _ = dq.shape
    n = T // tq
    k_tiles = dk.reshape(B, n, tq, DIFF_HEADS * LANES)
    vt_tiles = dv.reshape(B, n, tq, DIFF_HEADS, LANES).transpose(0, 3, 1, 4, 2)
    og_cols = jnp.broadcast_to(out_g.reshape(LANES, 1), (LANES, tq))
    width = DIFF_HEADS * LANES
    cst = lambda a: pl.BlockSpec(a.shape, lambda b, i: (0,) * a.ndim)
    return pl.pallas_call(
        functools.partial(_diff_kernel, tq=tq),
        grid=(B, n),
        in_specs=[pl.BlockSpec((1, tq, width), lambda b, i: (b, i, 0)),
                  pl.BlockSpec((1, n, tq, width), lambda b, i: (b, 0, 0, 0)),
                  pl.BlockSpec((1, DIFF_HEADS, n, LANES, tq), lambda b, i: (b, 0, 0, 0, 0)),
                  cst(lams[0]), cst(lams[1]), cst(lams[2]), cst(lams[3]), cst(og_cols)],
        out_specs=pl.BlockSpec((1, tq, width), lambda b, i: (b, i, 0)),
        out_shape=jax.ShapeDtypeStruct((B, T, width), BF16),
        compiler_params=pltpu.CompilerParams(
            dimension_semantics=("parallel", "parallel"), vmem_limit_bytes=VMEM_LIMIT),
        name="diff_attention",
    )(dq, k_tiles, vt_tiles, *lams, og_cols)


def _dsa_kernel(sq_ref, iq_ref, iw_ref, sk_ref, ik_ref, svt_ref, o_ref, key_ref, *, tq, tk, topk, t_bits):
    qi = pl.program_id(1)
    n_kv = (qi * tq + tq + tk - 1) // tk
    lane = lax.broadcasted_iota(I32, (tq, LANES), 1)
    lo_half = lane < HEAD_DIM
    krow = lax.broadcasted_iota(I32, (tk, tq), 0)
    q_chunk = (qi * tq + lax.broadcasted_iota(I32, (tk, tq), 1)) // CHUNK
    idx_scale = (IDX_HEADS ** -0.5) * (HEAD_DIM ** -0.5)
    key_neg_inf = jnp.int32(-2139095041)

    def all_heads(ref):
        out = []
        for p in range(4):
            slab = ref[0, :, p * LANES:(p + 1) * LANES]
            zero = jnp.zeros_like(slab)
            out.append(jnp.where(lo_half, slab, zero))
            out.append(jnp.where(lo_half, zero, slab))
        return jnp.concatenate(out, axis=0)

    def per_head(x, h):
        return x[:, h * tq:(h + 1) * tq]

    iq_all = all_heads(iq_ref)
    iw_t = iw_ref[0].T
    w_all = jnp.concatenate([iw_t[h:h + 1, :] for h in range(IDX_HEADS)], axis=1)

    def score_body(j, carry):
        r = jnp.maximum(_nt(ik_ref[0, j], iq_all), 0.0) * w_all
        score = per_head(r, 0)
        for h in range(1, IDX_HEADS):
            score = score + per_head(r, h)
        score = score * idx_scale + 0.0
        score = jnp.where((j * tk + krow) // CHUNK <= q_chunk, score, -jnp.inf)
        bits = lax.bitcast_convert_type(score, I32)
        key_ref[j] = bits ^ ((bits >> 31) & jnp.int32(0x7FFFFFFF))
        return carry

    lax.fori_loop(0, n_kv, score_body, 0)

    def count(pred_fn):
        def body(j, c):
            hit = jnp.where(pred_fn(key_ref[j], j * tk + krow), 1.0, 0.0)
            parts = [hit[8 * i:8 * i + 8] for i in range(tk // 8)]
            while len(parts) > 1:
                parts = [parts[i] + parts[i + 1] for i in range(0, len(parts), 2)]
            return c + parts[0]
        c = lax.fori_loop(0, n_kv, body, jnp.zeros((8, tq), F32))
        return jnp.sum(c, axis=0, keepdims=True)

    def bit_body(it, t):
        cand = t + jnp.left_shift(jnp.int32(1), 31 - it)
        c = count(lambda kj, _: kj >= cand)
        return jnp.where(c >= float(topk), cand, t)

    thr = lax.fori_loop(0, 32, bit_body, jnp.full((1, tq), jnp.iinfo(jnp.int32).min, I32))
    need = float(topk) - count(lambda kj, _: kj > thr)

    def tie_body(it, jmax):
        cand = jmax + jnp.left_shift(jnp.int32(1), t_bits - 1 - it)
        c = count(lambda kj, kidx: (kj == thr) & (kidx < cand))
        return jnp.where(c < need, cand, jmax)

    jmax = lax.fori_loop(0, t_bits, tie_body, jnp.zeros((1, tq), I32))

    q_all = all_heads(sq_ref)
    width = DSA_HEADS * tq

    def attn_body(j, carry):
        m, l, acc = carry
        kj = key_ref[j]
        sel = ((kj > thr) | ((kj == thr) & (j * tk + krow <= jmax))) & (kj != key_neg_inf)
        sel1 = jnp.where(sel, 1.0, 0.0)
        sel_all = jnp.concatenate([sel1] * DSA_HEADS, axis=1)
        s = jnp.where(sel_all > 0.5, _nt(sk_ref[0, j], q_all), NEG)
        m_new = jnp.maximum(m, jnp.max(s, axis=0, keepdims=True))
        alpha = jnp.exp(m - m_new)
        p = jnp.exp(s - m_new) * sel_all
        l = alpha * l + jnp.sum(p, axis=0, keepdims=True)
        acc = alpha * acc + jnp.dot(svt_ref[0, j], p.astype(BF16), preferred_element_type=F32)
        return m_new, l, acc

    init = (jnp.full((1, width), NEG, F32), jnp.zeros((1, width), F32), jnp.zeros((HEAD_DIM, width), F32))
    _, l, acc = lax.fori_loop(0, n_kv, attn_body, init)
    o_all = acc * (1.0 / l)
    o_t = jnp.concatenate([per_head(o_all, h) for h in range(DSA_HEADS)], axis=0)
    o_ref[0] = o_t.T.astype(o_ref.dtype)


def _dsa_attention(sq, iq, iw, skk, ikk, sv, tq, tk):
    B, T, _ = sq.shape
    n = T // tk
    topk = min(DSA_TOPK_MAX, T // 4)
    t_bits = max(1, (T - 1).bit_length())
    sk_tiles = skk.reshape(B, n, tk, LANES)
    ik_tiles = ikk.reshape(B, n, tk, LANES)
    svt_tiles = sv.reshape(B, n, tk, HEAD_DIM).transpose(0, 1, 3, 2)
    qblk = lambda c: pl.BlockSpec((1, tq, c), lambda b, i: (b, i, 0))
    kblk = lambda a: pl.BlockSpec((1,) + a.shape[1:], lambda b, i: (b, 0, 0, 0))
    return pl.pallas_call(
        functools.partial(_dsa_kernel, tq=tq, tk=tk, topk=topk, t_bits=t_bits),
        grid=(B, T // tq),
        in_specs=[qblk(512), qblk(512), qblk(LANES), kblk(sk_tiles), kblk(ik_tiles), kblk(svt_tiles)],
        out_specs=qblk(512),
        out_shape=jax.ShapeDtypeStruct((B, T, DSA_HEADS * HEAD_DIM), BF16),
        scratch_shapes=[pltpu.VMEM((n, tk, tq), I32)],
        compiler_params=pltpu.CompilerParams(dimension_semantics=("parallel", "parallel"),
                                             vmem_limit_bytes=VMEM_LIMIT),
        name="dsa_attention",
    )(sq, iq, iw, sk_tiles, ik_tiles, svt_tiles)


def _top_rows(s, rid, k, payload=None):
    big = float(2 ** 20)
    vals, ids, pay = [], [], []
    for _ in range(k):
        m = jnp.max(s, axis=0, keepdims=True)
        am = jnp.min(jnp.where(s == m, rid, big), axis=0, keepdims=True)
        hit = rid == am
        vals.append(m)
        ids.append(am)
        if payload is not None:
            pay.append(jnp.sum(jnp.where(hit, payload, 0.0), axis=0, keepdims=True))
        s = jnp.where(hit, -jnp.inf, s)
    cat = lambda xs: jnp.concatenate(xs, axis=0)
    return cat(vals), cat(ids), (cat(pay) if payload is not None else None)


def _pair_groups():
    groups = [(0, 0), (0, 8)] + [(a, 0) for a in range(1, 8)]
    return groups, [PEER_TOPK // (a + 1) for a, _ in groups]


def _pair_candidates(v1, i1, v2, i2):
    n = v1.shape[1]
    brow = lax.broadcasted_iota(I32, (8, n), 0)
    groups, limits = _pair_groups()
    cand, cidx, pos = [], [], []
    for (a, b0), lim in zip(groups, limits):
        ok = brow + b0 < lim
        cand.append(jnp.where(ok, v1[a:a + 1] + v2[b0:b0 + 8], -jnp.inf))
        cidx.append(i1[a:a + 1] * float(PEER_NKEYS) + i2[b0:b0 + 8])
        pos.append((brow + (a * PEER_TOPK + b0)).astype(F32))
    cand.append(v1[8:16] + v2[0:1])
    cidx.append(i1[8:16] * float(PEER_NKEYS) + i2[0:1])
    pos.append(((brow + 8) * PEER_TOPK).astype(F32))
    cat = lambda xs: jnp.concatenate(xs, axis=0)
    return cat(cand), cat(cidx), cat(pos)


def _mix_kernel(oa_ref, ob_ref, gate_ref, x_ref, g1_ref, sc_ref, sh_ref, n2_ref, wa_ref, wb_ref, wo_ref,
                wq_ref, sub_ref, x1_ref, h2_ref, eidx_ref, gw_ref, q_ref):
    ya = jnp.dot(oa_ref[0], wa_ref[...], preferred_element_type=F32)
    yb = jnp.dot(ob_ref[0], wb_ref[...], preferred_element_type=F32)
    D = ya.shape[1]
    tm = ya.shape[0]
    merged = gate_ref[0, :, :D].astype(F32) * ya + gate_ref[0, :, D:].astype(F32) * yb
    x1 = x_ref[0] + g1_ref[0] * jnp.dot(merged.astype(BF16), wo_ref[...], preferred_element_type=F32)
    x1_ref[0] = x1
    ms = jnp.mean(x1 * x1, axis=-1, keepdims=True)
    h2 = x1 * lax.rsqrt(ms + EPS) * n2_ref[...] * (1.0 + sc_ref[0]) + sh_ref[0]
    h2_ref[0] = h2
    q_ref[...] = jnp.dot(h2.astype(BF16), wq_ref[...], preferred_element_type=F32).astype(BF16)

    key_id = lax.broadcasted_iota(I32, (PEER_NKEYS, LANES), 0).astype(F32)
    for t in range(tm // LANES):
        cols = slice(t * LANES, (t + 1) * LANES)
        for h in range(PEER_HEADS):
            tops = []
            for p in range(2):
                hp = 2 * h + p
                s_t = _nt(sub_ref[hp], q_ref[cols, hp * LANES:(hp + 1) * LANES])
                v, i, _ = _top_rows(s_t, key_id, PEER_TOPK)
                tops.append((v, i))
            cand, cidx, pos = _pair_candidates(*tops[0], *tops[1])
            sc, _, e = _top_rows(cand, pos, PEER_TOPK, payload=cidx)
            ex = jnp.exp(sc - sc[0:1])
            g = ex / jnp.sum(ex, axis=0, keepdims=True)
            eidx_ref[h * PEER_TOPK:(h + 1) * PEER_TOPK, cols] = e.astype(I32)
            gw_ref[h * PEER_TOPK:(h + 1) * PEER_TOPK, cols] = g


def _mix(o_a, o_b, gates, x, g1, sc2, sh2, n2g, wa, wb, wo, wq, sub, tm):
    B, T, D = x.shape
    blk = lambda n: pl.BlockSpec((1, tm, n), lambda b, i: (b, i, 0))
    cst = lambda a: pl.BlockSpec(a.shape, lambda b, i: (0,) * a.ndim)
    mod = pl.BlockSpec((1, 1, D), lambda b, i: (b, 0, 0))
    nblk = T // tm
    tblk = pl.BlockSpec((PEER_PAIRS, tm), lambda b, i: (0, b * nblk + i))
    return pl.pallas_call(
        _mix_kernel,
        grid=(B, nblk),
        in_specs=[blk(512), blk(512), blk(2 * D), blk(D), mod, mod, mod, cst(n2g), cst(wa), cst(wb),
                  cst(wo), cst(wq), cst(sub)],
        out_specs=[blk(D), blk(D), tblk, tblk],
        out_shape=[jax.ShapeDtypeStruct((B, T, D), F32), jax.ShapeDtypeStruct((B, T, D), F32),
                   jax.ShapeDtypeStruct((PEER_PAIRS, B * T), I32),
                   jax.ShapeDtypeStruct((PEER_PAIRS, B * T), F32)],
        scratch_shapes=[pltpu.VMEM((tm, wq.shape[1]), BF16)],
        compiler_params=pltpu.CompilerParams(dimension_semantics=("parallel", "parallel"),
                                             vmem_limit_bytes=VMEM_LIMIT),
        name="mix_route",
    )(o_a, o_b, gates, x, g1, sc2, sh2, n2g, wa, wb, wo, wq, sub)


SLAB = 4
ROW_TILES = 8
GATHER_LEAD = 1
N_GATHER_BUFS = 2 * GATHER_LEAD
TOKENS_PER_TRIP = 8


def _gather_scratch():
    return [pltpu.VMEM((PEER_PAIRS * SLAB, LANES), I32) for _ in range(N_GATHER_BUFS)]


def _pack_table(t):
    n, d = t.shape
    tb = t.astype(BF16).reshape(n, SLAB, 2, LANES).transpose(0, 1, 3, 2)
    return lax.bitcast_convert_type(tb, I32).reshape(n * SLAB, LANES)


def _resident_table_spec(tab):
    return pl.BlockSpec(tab.shape, lambda i: (0, 0), pipeline_mode=pl.Buffered(1))


def _gather_rows(i, idx_ref, tab_ref, w32_ref):
    for k in range(PEER_PAIRS):
        r = pl.multiple_of(idx_ref[i, k], SLAB)
        w32_ref[k * SLAB:(k + 1) * SLAB, :] = tab_ref[pl.ds(r, SLAB), :]


def _skewed_token_loop(tb, gather, compute, bufs):
    n = len(bufs)
    for i in range(GATHER_LEAD):
        gather(i, bufs[i])

    def group(j, carry):
        i0 = TOKENS_PER_TRIP * j
        for u in range(TOKENS_PER_TRIP):
            gather(jnp.minimum(i0 + u + GATHER_LEAD, tb - 1), bufs[(u + GATHER_LEAD) % n])
            compute(i0 + u, bufs[u % n])
        return carry

    lax.fori_loop(0, tb // TOKENS_PER_TRIP, group, 0)


def _peer_u_kernel(idx_ref, h2_ref, g_ref, diag_ref, gsum_ref, tab_ref, w_ref, z_ref, *bufs, tb):
    diag = diag_ref[...]

    def compute(i, w32_ref):
        wb = pltpu.bitcast(w32_ref[...], BF16)
        hi, lo = _split_bf16(h2_ref[i])
        o = _nt(jnp.concatenate([hi, lo], axis=0), wb)
        z_ref[pl.ds(pl.multiple_of(i * ROW_TILES, ROW_TILES), ROW_TILES), :] = (o[:ROW_TILES] + o[ROW_TILES:]) * diag

    _skewed_token_loop(tb, lambda i, buf: _gather_rows(i, idx_ref, tab_ref, buf), compute, bufs)
    a8 = _dot2(z_ref[...], gsum_ref[...])
    a = jnp.sum(a8.reshape(tb, ROW_TILES, PEER_PAIRS), axis=1)
    gelu = 0.5 * a * (1.0 + lax.erf(a * (2.0 ** -0.5)))
    w_ref[...] = g_ref[...] * gelu


def _peer_v_kernel(idx_ref, w_ref, x1_ref, g2_ref, diag_ref, gexp_ref, tab_ref, o_ref, wrep_ref, *bufs, tb):
    diag = diag_ref[...]
    wrep_ref[...] = _dot2(w_ref[...], gexp_ref[...])
    g2 = g2_ref[0]

    def compute(i, w32_ref):
        wb = pltpu.bitcast(w32_ref[...], BF16)
        a = wrep_ref[pl.ds(i, 1), :] * diag
        hi, lo = _split_bf16(a)
        o = jnp.dot(jnp.concatenate([hi, lo], axis=0), wb, preferred_element_type=F32)
        o_ref[i] = x1_ref[i] + g2 * (o[:ROW_TILES] + o[ROW_TILES:])

    _skewed_token_loop(tb, lambda i, buf: _gather_rows(i, idx_ref, tab_ref, buf), compute, bufs)


def _peer_consts():
    c = np.arange(PEER_PAIRS * ROW_TILES)
    diag = (c[None, :] % ROW_TILES == np.arange(ROW_TILES)[:, None]).astype(np.float32)
    gsum = (c[:, None] // ROW_TILES == np.arange(PEER_PAIRS)[None, :]).astype(np.float32)
    return jnp.asarray(diag), jnp.asarray(gsum, BF16), jnp.asarray(gsum.T, BF16)


def _peer_u(idx4, h2r, g, utab, tb):
    N = h2r.shape[0]
    diag, gsum, _ = _peer_consts()
    cst = lambda a: pl.BlockSpec(a.shape, lambda i: (0,) * a.ndim)
    return pl.pallas_call(
        functools.partial(_peer_u_kernel, tb=tb),
        grid=(N // tb,),
        in_specs=[pl.BlockSpec((tb, PEER_PAIRS), lambda i: (i, 0), memory_space=pltpu.SMEM),
                  pl.BlockSpec((tb, ROW_TILES, LANES), lambda i: (i, 0, 0)),
                  pl.BlockSpec((tb, PEER_PAIRS), lambda i: (i, 0)),
                  cst(diag), cst(gsum), _resident_table_spec(utab)],
        out_specs=pl.BlockSpec((tb, PEER_PAIRS), lambda i: (i, 0)),
        out_shape=jax.ShapeDtypeStruct((N, PEER_PAIRS), F32),
        scratch_shapes=[pltpu.VMEM((tb * ROW_TILES, PEER_PAIRS * ROW_TILES), F32)] + _gather_scratch(),
        compiler_params=pltpu.CompilerParams(dimension_semantics=("arbitrary",),
                                             vmem_limit_bytes=VMEM_LIMIT),
        name="peer_u",
    )(idx4, h2r, g, diag, gsum, utab)


def _peer_v(idx4, w, x1r, g2r, vtab, tb, blocks_per_batch):
    N = x1r.shape[0]
    diag, _, gexp = _peer_consts()
    cst = lambda a: pl.BlockSpec(a.shape, lambda i: (0,) * a.ndim)
    return pl.pallas_call(
        functools.partial(_peer_v_kernel, tb=tb),
        grid=(N // tb,),
        in_specs=[pl.BlockSpec((tb, PEER_PAIRS), lambda i: (i, 0), memory_space=pltpu.SMEM),
                  pl.BlockSpec((tb, PEER_PAIRS), lambda i: (i, 0)),
                  pl.BlockSpec((tb, ROW_TILES, LANES), lambda i: (i, 0, 0)),
                  pl.BlockSpec((1, ROW_TILES, LANES), lambda i: (i // blocks_per_batch, 0, 0)),
                  cst(diag), cst(gexp), _resident_table_spec(vtab)],
        out_specs=pl.BlockSpec((tb, ROW_TILES, LANES), lambda i: (i, 0, 0)),
        out_shape=jax.ShapeDtypeStruct((N, ROW_TILES, LANES), F32),
        scratch_shapes=[pltpu.VMEM((tb, PEER_PAIRS * ROW_TILES), F32)] + _gather_scratch(),
        compiler_params=pltpu.CompilerParams(dimension_semantics=("arbitrary",),
                                             vmem_limit_bytes=VMEM_LIMIT),
        name="peer_v",
    )(idx4, w, x1r, g2r, diag, gexp, vtab)


def _permute_w_in(w):
    o = np.cumsum([0, 512, 512, 512, 512, 64, 64, 512, 64, 8, 2048])
    dq, dk, dv, sq, sk, sv, iq, ik, iw, gate = [w[:, o[i]:o[i + 1]] for i in range(10)]
    pad = jnp.zeros((w.shape[0], LANES - IDX_HEADS), w.dtype)
    return jnp.concatenate([dq, dk, dv, sq, iq, sk, sk, ik, ik, sv, sv, iw, pad, gate], axis=1)


def _rope_consts(diff_q_g, diff_k_g, dsa_q_g, dsa_k_g):
    inv = ROPE_THETA ** (-(jnp.arange(0, HEAD_DIM, 2, dtype=F32) / HEAD_DIM))
    inv128 = jnp.tile(inv, LANES // HALF).reshape(1, LANES)
    sgn = np.where(np.arange(LANES) % HEAD_DIM < HALF, -1.0, 1.0).astype(np.float32).reshape(1, LANES)
    grp = np.arange(LANES) // HEAD_DIM
    gmat = jnp.asarray((grp[:, None] == grp[None, :]).astype(np.float32) / HEAD_DIM, BF16)
    t8 = lambda g: jnp.tile(g, 8).reshape(1, 512)
    return (inv128, jnp.asarray(sgn), t8(diff_q_g), t8(diff_k_g), t8(dsa_q_g),
            jnp.tile(dsa_k_g, 2).reshape(1, LANES), gmat)


def kernel(x, c, positions, w_ada, b_ada, norm1_g, w_in, diff_q_g, diff_k_g, diff_lam_q1, diff_lam_k1, diff_lam_q2, diff_lam_k2, diff_out_g, dsa_q_g, dsa_k_g, w_branch_a, w_branch_b, w_out, norm2_g, peer_w_q, peer_sub_keys, peer_u, peer_v):
    B, T, D = x.shape
    N = B * T
    assert w_ada.shape[0] == 1, "single-layer block"
    tm = min(256, T)
    tq_diff = min(256, T)
    tq_dsa = min(128, T)
    tk_dsa = min(256, T)
    tb = 128

    mod = _ada(c, w_ada[0], b_ada[0])
    sh1, sc1, g1, sh2, sc2, g2 = [m.reshape(B, 1, D) for m in jnp.split(mod, 6, axis=-1)]

    consts = _rope_consts(diff_q_g[0], diff_k_g[0], dsa_q_g[0], dsa_k_g[0])
    w_perm = _permute_w_in(w_in[0]).astype(BF16)
    dq, dk, dv, sq, iq, skk, ikk, svv, iw, gates = _inproj(
        x, sc1, sh1, norm1_g, positions.reshape(B, T, 1), consts, w_perm, tm)

    lams = [v.reshape(1, HEAD_DIM) for v in (diff_lam_q1[0], diff_lam_k1[0], diff_lam_q2[0], diff_lam_k2[0])]
    o_a = _diff_attention(dq, dk, dv, lams, diff_out_g[0], tq_diff)
    o_b = _dsa_attention(sq, iq, iw, skk, ikk, svv[..., :HEAD_DIM], tq_dsa, tk_dsa)

    sub = peer_sub_keys[0].reshape(2 * PEER_HEADS, PEER_NKEYS, LANES).astype(BF16)
    x1, h2, eidx_t, gw_t = _mix(o_a, o_b, gates, x, g1, sc2, sh2, norm2_g,
                                w_branch_a[0].astype(BF16), w_branch_b[0].astype(BF16),
                                w_out[0].astype(BF16), peer_w_q[0].astype(BF16), sub, tm)

    idx4 = eidx_t.T * SLAB
    gw = gw_t.T
    w = _peer_u(idx4, h2.reshape(N, ROW_TILES, LANES), gw, _pack_table(peer_u[0]), tb)
    out = _peer_v(idx4, w, x1.reshape(N, ROW_TILES, LANES), g2.reshape(B, ROW_TILES, LANES),
                  _pack_table(peer_v[0]), tb, T // tb)
    return out.reshape(B, T, D)
```

```python
import functools
import math

import numpy as np
import jax
import jax.numpy as jnp
from jax import lax
from jax.experimental import pallas as pl
from jax.experimental.pallas import tpu as pltpu

F32 = jnp.float32
BF16 = jnp.bfloat16
I32 = jnp.int32

LANES = 128
HEAD_DIM = 64
HALF = HEAD_DIM // 2
CHUNK = 64
ROPE_THETA = 10000.0
EPS = 1e-6
DIFF_HEADS = 4
DSA_HEADS = 8
IDX_HEADS = 8
DSA_TOPK_MAX = 256
PEER_HEADS = 8
PEER_NKEYS = 128
PEER_TOPK = 16
PEER_PAIRS = PEER_HEADS * PEER_TOPK
LAMBDA_INIT = 0.8 - 0.6 * math.exp(-0.3 * 0)
NEG = -1e30
VMEM_LIMIT = 56 * 1024 * 1024

C_DQ, C_DK, C_DV, C_SQ, C_IQ = 0, 512, 1024, 1536, 2048
C_SK, C_IK, C_SV, C_IW, C_GATE = 2560, 2688, 2816, 2944, 3072
N_COLS = C_GATE + 2048


def _nt(a, b):
    return lax.dot_general(a, b, (((1,), (1,)), ((), ())), preferred_element_type=F32)


def _split_bf16(x):
    hi = x.astype(BF16)
    lo = (x - hi.astype(F32)).astype(BF16)
    return hi, lo


def _dot2(x, w):
    hi, lo = _split_bf16(x)
    return (jnp.dot(hi, w, preferred_element_type=F32)
            + jnp.dot(lo, w, preferred_element_type=F32))


def _ada_kernel(c_ref, w_ref, b_ref, o_ref):
    c = c_ref[...]
    s = c * jax.nn.sigmoid(c)
    o_ref[...] = jnp.dot(s, w_ref[...], preferred_element_type=F32,
                         precision=lax.Precision.HIGHEST) + b_ref[...]


def _ada(c, w, b):
    B, D = c.shape
    n = w.shape[1]
    bn = 1024
    return pl.pallas_call(
        _ada_kernel,
        grid=(n // bn,),
        in_specs=[pl.BlockSpec((B, D), lambda j: (0, 0)),
                  pl.BlockSpec((D, bn), lambda j: (0, j)),
                  pl.BlockSpec((1, bn), lambda j: (0, j))],
        out_specs=pl.BlockSpec((B, bn), lambda j: (0, j)),
        out_shape=jax.ShapeDtypeStruct((B, n), F32),
        compiler_params=pltpu.CompilerParams(dimension_semantics=("parallel",),
                                             vmem_limit_bytes=VMEM_LIMIT),
        name="ada",
    )(c, w, b.reshape(1, n))


def _rope128(y, cos, sin_signed, first_half):
    partner = jnp.where(first_half, pltpu.roll(y, LANES - HALF, 1), pltpu.roll(y, HALF, 1))
    return y * cos + partner * sin_signed


def _group_ms(y, gmat):
    return _dot2(y * y, gmat)


def _inproj_kernel(x_ref, sc_ref, sh_ref, n1_ref, pos_ref, inv_ref, sgn_ref, gq_ref, gk_ref,
                   gsq_ref, gsk_ref, gmat_ref, w_ref,
                   dq_ref, dk_ref, dv_ref, sq_ref, iq_ref, sk_ref, ik_ref, sv_ref, iw_ref, gate_ref):
    x = x_ref[0]
    ms = jnp.mean(x * x, axis=-1, keepdims=True)
    h = x * lax.rsqrt(ms + EPS) * n1_ref[...] * (1.0 + sc_ref[0]) + sh_ref[0]
    hb = h.astype(BF16)
    tm = x.shape[0]

    ang = pos_ref[0].astype(F32) * inv_ref[...]
    cos = jnp.cos(ang)
    sin = jnp.sin(ang) * sgn_ref[...]
    lane = lax.broadcasted_iota(I32, (tm, LANES), 1)
    first_half = (lane % HEAD_DIM) < HALF
    gmat = gmat_ref[...]

    def proj(c0, n):
        return jnp.dot(hb, w_ref[:, c0:c0 + n], preferred_element_type=F32)

    def normed_rope(c0, nblk, gain_ref, out_ref, scale):
        for i in range(nblk):
            y = proj(c0 + i * LANES, LANES)
            if gain_ref is not None:
                y = y * lax.rsqrt(_group_ms(y, gmat) + EPS) * gain_ref[:, i * LANES:(i + 1) * LANES]
            y = _rope128(y, cos, sin, first_half)
            if scale != 1.0:
                y = y * scale
            out_ref[0, :, i * LANES:(i + 1) * LANES] = y.astype(out_ref.dtype)

    qscale = HEAD_DIM ** -0.5
    normed_rope(C_DQ, 4, gq_ref, dq_ref, qscale)
    normed_rope(C_DK, 4, gk_ref, dk_ref, 1.0)
    dv_ref[0] = proj(C_DV, 512).astype(dv_ref.dtype)
    normed_rope(C_SQ, 4, gsq_ref, sq_ref, qscale)
    normed_rope(C_IQ, 4, None, iq_ref, 1.0)
    normed_rope(C_SK, 1, gsk_ref, sk_ref, 1.0)
    normed_rope(C_IK, 1, None, ik_ref, 1.0)
    sv_ref[0] = proj(C_SV, LANES).astype(sv_ref.dtype)
    iw_ref[0] = proj(C_IW, LANES)
    for i in range(4):
        g = proj(C_GATE + i * 512, 512)
        gate_ref[0, :, i * 512:(i + 1) * 512] = jax.nn.sigmoid(g).astype(gate_ref.dtype)


def _inproj(x, sc1, sh1, n1g, pos, consts, w_perm, tm):
    B, T, D = x.shape
    inv128, sgn128, gq, gk, gsq, gsk, gmat = consts
    tok = lambda n, dt: jax.ShapeDtypeStruct((B, T, n), dt)
    blk = lambda n: pl.BlockSpec((1, tm, n), lambda b, i: (b, i, 0))
    cst = lambda a: pl.BlockSpec(a.shape, lambda b, i: (0,) * a.ndim)
    mod = pl.BlockSpec((1, 1, D), lambda b, i: (b, 0, 0))
    return pl.pallas_call(
        _inproj_kernel,
        grid=(B, T // tm),
        in_specs=[blk(D), mod, mod, cst(n1g), blk(1), cst(inv128), cst(sgn128), cst(gq), cst(gk),
                  cst(gsq), cst(gsk), cst(gmat), cst(w_perm)],
        out_specs=[blk(512), blk(512), blk(512), blk(512), blk(512), blk(128), blk(128), blk(128),
                   blk(128), blk(2048)],
        out_shape=[tok(512, BF16), tok(512, BF16), tok(512, BF16), tok(512, BF16), tok(512, BF16),
                   tok(128, BF16), tok(128, BF16), tok(128, BF16), tok(128, F32), tok(2048, BF16)],
        compiler_params=pltpu.CompilerParams(dimension_semantics=("parallel", "parallel"),
                                             vmem_limit_bytes=VMEM_LIMIT),
        name="inproj",
    )(x, sc1, sh1, n1g, pos, inv128, sgn128, gq, gk, gsq, gsk, gmat, w_perm)


def _diff_kernel(q_ref, k_ref, vt_ref, lq1_ref, lk1_ref, lq2_ref, lk2_ref, og_ref, o_ref, *, tq):
    qi = pl.program_id(1)
    lane = lax.broadcasted_iota(I32, (tq, LANES), 1)
    q2 = []
    for h in range(DIFF_HEADS):
        q = q_ref[0, :, h * LANES:(h + 1) * LANES]
        zero = jnp.zeros_like(q)
        q2.append(jnp.concatenate([jnp.where(lane < HEAD_DIM, q, zero),
                                   jnp.where(lane >= HEAD_DIM, q, zero)], axis=0))

    def tile(j, carry, on_diagonal):
        out = []
        for h in range(DIFF_HEADS):
            m, l, acc = carry[h]
            s = _nt(k_ref[0, j, :, h * LANES:(h + 1) * LANES], q2[h])
            if on_diagonal:
                krow = lax.broadcasted_iota(I32, (tq, 2 * tq), 0)
                qcol = lax.broadcasted_iota(I32, (tq, 2 * tq), 1)
                qcol = jnp.where(qcol >= tq, qcol - tq, qcol)
                s = jnp.where(krow // CHUNK <= qcol // CHUNK, s, NEG)
            m_new = jnp.maximum(m, jnp.max(s, axis=0, keepdims=True))
            alpha = jnp.exp(m - m_new)
            p = jnp.exp(s - m_new)
            l = alpha * l + jnp.sum(p, axis=0, keepdims=True)
            acc = alpha * acc + jnp.dot(vt_ref[0, h, j], p.astype(BF16), preferred_element_type=F32)
            out.append((m_new, l, acc))
        return tuple(out)

    init = tuple((jnp.full((1, 2 * tq), NEG, F32), jnp.zeros((1, 2 * tq), F32),
                  jnp.zeros((LANES, 2 * tq), F32)) for _ in range(DIFF_HEADS))
    carry = lax.fori_loop(0, qi, lambda j, c: tile(j, c, False), init)
    res = tile(qi, carry, True)
    lam = (jnp.exp(jnp.sum(lq1_ref[...] * lk1_ref[...], axis=-1, keepdims=True))
           - jnp.exp(jnp.sum(lq2_ref[...] * lk2_ref[...], axis=-1, keepdims=True)) + LAMBDA_INIT)
    for h in range(DIFF_HEADS):
        _, l, acc = res[h]
        o = acc * (1.0 / l)
        o = o[:, :tq] - lam * o[:, tq:]
        ms = jnp.mean(o * o, axis=0, keepdims=True)
        o = o * lax.rsqrt(ms + EPS) * og_ref[...] * (1.0 - LAMBDA_INIT)
        o_ref[0, :, h * LANES:(h + 1) * LANES] = o.T.astype(o_ref.dtype)


def _diff_attention(dq, dk, dv, lams, out_g, tq):
    B, T, _ = dq.shape
    n = T // tq
    k_tiles = dk.reshape(B, n, tq, DIFF_HEADS * LANES)
    vt_tiles = dv.reshape(B, n, tq, DIFF_HEADS, LANES).transpose(0, 3, 1, 4, 2)
    og_cols = jnp.broadcast_to(out_g.reshape(LANES, 1), (LANES, tq))
    width = DIFF_HEADS * LANES
    cst = lambda a: pl.BlockSpec(a.shape, lambda b, i: (0,) * a.ndim)
    return pl.pallas_call(
        functools.partial(_diff_kernel, tq=tq),
        grid=(B, n),
        in_specs=[pl.BlockSpec((1, tq, width), lambda b, i: (b, i, 0)),
                  pl.BlockSpec((1, n, tq, width), lambda b, i: (b, 0, 0, 0)),
                  pl.BlockSpec((1, DIFF_HEADS, n, LANES, tq), lambda b, i: (b, 0, 0, 0, 0)),
                  cst(lams[0]), cst(lams[1]), cst(lams[2]), cst(lams[3]), cst(og_cols)],
        out_specs=pl.BlockSpec((1, tq, width), lambda b, i: (b, i, 0)),
        out_shape=jax.ShapeDtypeStruct((B, T, width), BF16),
        compiler_params=pltpu.CompilerParams(
            dimension_semantics=("parallel", "parallel"), vmem_limit_bytes=VMEM_LIMIT),
        name="diff_attention",
    )(dq, k_tiles, vt_tiles, *lams, og_cols)


def _dsa_kernel(sq_ref, iq_ref, iw_ref, sk_ref, ik_ref, svt_ref, o_ref, key_ref, *, tq, tk, topk, t_bits):
    qi = pl.program_id(1)
    n_kv = (qi * tq + tq + tk - 1) // tk
    lane = lax.broadcasted_iota(I32, (tq, LANES), 1)
    lo_half = lane < HEAD_DIM
    krow = lax.broadcasted_iota(I32, (tk, tq), 0)
    q_chunk = (qi * tq + lax.broadcasted_iota(I32, (tk, tq), 1)) // CHUNK
    idx_scale = (IDX_HEADS ** -0.5) * (HEAD_DIM ** -0.5)
    key_neg_inf = jnp.int32(-2139095041)

    def all_heads(ref):
        out = []
        for p in range(4):
            slab = ref[0, :, p * LANES:(p + 1) * LANES]
            zero = jnp.zeros_like(slab)
            out.append(jnp.where(lo_half, slab, zero))
            out.append(jnp.where(lo_half, zero, slab))
        return jnp.concatenate(out, axis=0)

    def per_head(x, h):
        return x[:, h * tq:(h + 1) * tq]

    iq_all = all_heads(iq_ref)
    iw_t = iw_ref[0].T
    w_all = jnp.concatenate([iw_t[h:h + 1, :] for h in range(IDX_HEADS)], axis=1)

    def score_body(j, carry):
        r = jnp.maximum(_nt(ik_ref[0, j], iq_all), 0.0) * w_all
        score = per_head(r, 0)
        for h in range(1, IDX_HEADS):
            score = score + per_head(r, h)
        score = score * idx_scale + 0.0
        score = jnp.where((j * tk + krow) // CHUNK <= q_chunk, score, -jnp.inf)
        bits = lax.bitcast_convert_type(score, I32)
        key_ref[j] = bits ^ ((bits >> 31) & jnp.int32(0x7FFFFFFF))
        return carry

    lax.fori_loop(0, n_kv, score_body, 0)

    def count(pred_fn):
        def body(j, c):
            hit = jnp.where(pred_fn(key_ref[j], j * tk + krow), 1.0, 0.0)
            parts = [hit[8 * i:8 * i + 8] for i in range(tk // 8)]
            while len(parts) > 1:
                parts = [parts[i] + parts[i + 1] for i in range(0, len(parts), 2)]
            return c + parts[0]
        c = lax.fori_loop(0, n_kv, body, jnp.zeros((8, tq), F32))
        return jnp.sum(c, axis=0, keepdims=True)

    def bit_body(it, t):
        cand = t + jnp.left_shift(jnp.int32(1), 31 - it)
        c = count(lambda kj, _: kj >= cand)
        return jnp.where(c >= float(topk), cand, t)

    thr = lax.fori_loop(0, 32, bit_body, jnp.full((1, tq), jnp.iinfo(jnp.int32).min, I32))
    need = float(topk) - count(lambda kj, _: kj > thr)

    def tie_body(it, jmax):
        cand = jmax + jnp.left_shift(jnp.int32(1), t_bits - 1 - it)
        c = count(lambda kj, kidx: (kj == thr) & (kidx < cand))
        return jnp.where(c < need, cand, jmax)

    jmax = lax.fori_loop(0, t_bits, tie_body, jnp.zeros((1, tq), I32))

    q_all = all_heads(sq_ref)
    width = DSA_HEADS * tq

    def attn_body(j, carry):
        m, l, acc = carry
        kj = key_ref[j]
        sel = ((kj > thr) | ((kj == thr) & (j * tk + krow <= jmax))) & (kj != key_neg_inf)
        sel1 = jnp.where(sel, 1.0, 0.0)
        sel_all = jnp.concatenate([sel1] * DSA_HEADS, axis=1)
        s = jnp.where(sel_all > 0.5, _nt(sk_ref[0, j], q_all), NEG)
        m_new = jnp.maximum(m, jnp.max(s, axis=0, keepdims=True))
        alpha = jnp.exp(m - m_new)
        p = jnp.exp(s - m_new) * sel_all
        l = alpha * l + jnp.sum(p, axis=0, keepdims=True)
        acc = alpha * acc + jnp.dot(svt_ref[0, j], p.astype(BF16), preferred_element_type=F32)
        return m_new, l, acc

    init = (jnp.full((1, width), NEG, F32), jnp.zeros((1, width), F32), jnp.zeros((HEAD_DIM, width), F32))
    _, l, acc = lax.fori_loop(0, n_kv, attn_body, init)
    o_all = acc * (1.0 / l)
    o_t = jnp.concatenate([per_head(o_all, h) for h in range(DSA_HEADS)], axis=0)
    o_ref[0] = o_t.T.astype(o_ref.dtype)


def _dsa_attention(sq, iq, iw, skk, ikk, sv, tq, tk):
    B, T, _ = sq.shape
    n = T // tk
    topk = min(DSA_TOPK_MAX, T // 4)
    t_bits = max(1, (T - 1).bit_length())
    sk_tiles = skk.reshape(B, n, tk, LANES)
    ik_tiles = ikk.reshape(B, n, tk, LANES)
    svt_tiles = sv.reshape(B, n, tk, HEAD_DIM).transpose(0, 1, 3, 2)
    qblk = lambda c: pl.BlockSpec((1, tq, c), lambda b, i: (b, i, 0))
    kblk = lambda a: pl.BlockSpec((1,) + a.shape[1:], lambda b, i: (b, 0, 0, 0))
    return pl.pallas_call(
        functools.partial(_dsa_kernel, tq=tq, tk=tk, topk=topk, t_bits=t_bits),
        grid=(B, T // tq),
        in_specs=[qblk(512), qblk(512), qblk(LANES), kblk(sk_tiles), kblk(ik_tiles), kblk(svt_tiles)],
        out_specs=qblk(512),
        out_shape=jax.ShapeDtypeStruct((B, T, DSA_HEADS * HEAD_DIM), BF16),
        scratch_shapes=[pltpu.VMEM((n, tk, tq), I32)],
        compiler_params=pltpu.CompilerParams(dimension_semantics=("parallel", "parallel"),
                                             vmem_limit_bytes=VMEM_LIMIT),
        name="dsa_attention",
    )(sq, iq, iw, sk_tiles, ik_tiles, svt_tiles)


def _top_rows(s, rid, k, payload=None):
    big = float(2 ** 20)
    vals, ids, pay = [], [], []
    for _ in range(k):
        m = jnp.max(s, axis=0, keepdims=True)
        am = jnp.min(jnp.where(s == m, rid, big), axis=0, keepdims=True)
        hit = rid == am
        vals.append(m)
        ids.append(am)
        if payload is not None:
            pay.append(jnp.sum(jnp.where(hit, payload, 0.0), axis=0, keepdims=True))
        s = jnp.where(hit, -jnp.inf, s)
    cat = lambda xs: jnp.concatenate(xs, axis=0)
    return cat(vals), cat(ids), (cat(pay) if payload is not None else None)


def _pair_groups():
    groups = [(0, 0), (0, 8)] + [(a, 0) for a in range(1, 8)]
    return groups, [PEER_TOPK // (a + 1) for a, _ in groups]


def _pair_candidates(v1, i1, v2, i2):
    n = v1.shape[1]
    brow = lax.broadcasted_iota(I32, (8, n), 0)
    groups, limits = _pair_groups()
    cand, cidx, pos = [], [], []
    for (a, b0), lim in zip(groups, limits):
        ok = brow + b0 < lim
        cand.append(jnp.where(ok, v1[a:a + 1] + v2[b0:b0 + 8], -jnp.inf))
        cidx.append(i1[a:a + 1] * float(PEER_NKEYS) + i2[b0:b0 + 8])
        pos.append((brow + (a * PEER_TOPK + b0)).astype(F32))
    cand.append(v1[8:16] + v2[0:1])
    cidx.append(i1[8:16] * float(PEER_NKEYS) + i2[0:1])
    pos.append(((brow + 8) * PEER_TOPK).astype(F32))
    cat = lambda xs: jnp.concatenate(xs, axis=0)
    return cat(cand), cat(cidx), cat(pos)


def _mix_kernel(oa_ref, ob_ref, gate_ref, x_ref, g1_ref, sc_ref, sh_ref, n2_ref, wa_ref, wb_ref, wo_ref,
                wq_ref, sub_ref, x1_ref, h2_ref, eidx_ref, gw_ref, q_ref):
    ya = jnp.dot(oa_ref[0], wa_ref[...], preferred_element_type=F32)
    yb = jnp.dot(ob_ref[0], wb_ref[...], preferred_element_type=F32)
    D = ya.shape[1]
    tm = ya.shape[0]
    merged = gate_ref[0, :, :D].astype(F32) * ya + gate_ref[0, :, D:].astype(F32) * yb
    x1 = x_ref[0] + g1_ref[0] * jnp.dot(merged.astype(BF16), wo_ref[...], preferred_element_type=F32)
    x1_ref[0] = x1
    ms = jnp.mean(x1 * x1, axis=-1, keepdims=True)
    h2 = x1 * lax.rsqrt(ms + EPS) * n2_ref[...] * (1.0 + sc_ref[0]) + sh_ref[0]
    h2_ref[0] = h2
    q_ref[...] = jnp.dot(h2.astype(BF16), wq_ref[...], preferred_element_type=F32).astype(BF16)

    key_id = lax.broadcasted_iota(I32, (PEER_NKEYS, LANES), 0).astype(F32)
    for t in range(tm // LANES):
        cols = slice(t * LANES, (t + 1) * LANES)
        for h in range(PEER_HEADS):
            tops = []
            for p in range(2):
                hp = 2 * h + p
                s_t = _nt(sub_ref[hp], q_ref[cols, hp * LANES:(hp + 1) * LANES])
                v, i, _ = _top_rows(s_t, key_id, PEER_TOPK)
                tops.append((v, i))
            cand, cidx, pos = _pair_candidates(*tops[0], *tops[1])
            sc, _, e = _top_rows(cand, pos, PEER_TOPK, payload=cidx)
            ex = jnp.exp(sc - sc[0:1])
            g = ex / jnp.sum(ex, axis=0, keepdims=True)
            eidx_ref[h * PEER_TOPK:(h + 1) * PEER_TOPK, cols] = e.astype(I32)
            gw_ref[h * PEER_TOPK:(h + 1) * PEER_TOPK, cols] = g


def _mix(o_a, o_b, gates, x, g1, sc2, sh2, n2g, wa, wb, wo, wq, sub, tm):
    B, T, D = x.shape
    blk = lambda n: pl.BlockSpec((1, tm, n), lambda b, i: (b, i, 0))
    cst = lambda a: pl.BlockSpec(a.shape, lambda b, i: (0,) * a.ndim)
    mod = pl.BlockSpec((1, 1, D), lambda b, i: (b, 0, 0))
    nblk = T // tm
    tblk = pl.BlockSpec((PEER_PAIRS, tm), lambda b, i: (0, b * nblk + i))
    return pl.pallas_call(
        _mix_kernel,
        grid=(B, nblk),
        in_specs=[blk(512), blk(512), blk(2 * D), blk(D), mod, mod, mod, cst(n2g), cst(wa), cst(wb),
                  cst(wo), cst(wq), cst(sub)],
        out_specs=[blk(D), blk(D), tblk, tblk],
        out_shape=[jax.ShapeDtypeStruct((B, T, D), F32), jax.ShapeDtypeStruct((B, T, D), F32),
                   jax.ShapeDtypeStruct((PEER_PAIRS, B * T), I32),
                   jax.ShapeDtypeStruct((PEER_PAIRS, B * T), F32)],
        scratch_shapes=[pltpu.VMEM((tm, wq.shape[1]), BF16)],
        compiler_params=pltpu.CompilerParams(dimension_semantics=("parallel", "parallel"),
                                             vmem_limit_bytes=VMEM_LIMIT),
        name="mix_route",
    )(o_a, o_b, gates, x, g1, sc2, sh2, n2g, wa, wb, wo, wq, sub)


SLAB = 4
ROW_TILES = 8
GROUP = 8
RING = 4


def _peer_scratch():
    return ([pltpu.SMEM((1, GROUP * PEER_PAIRS), I32) for _ in range(RING)] + [pltpu.SemaphoreType.DMA((RING,))]
            + [pltpu.VMEM((PEER_PAIRS * SLAB, LANES), I32) for _ in range(2)])


def _pack_table(t):
    n, d = t.shape
    tb = t.astype(BF16).reshape(n, SLAB, 2, LANES).transpose(0, 1, 3, 2)
    return lax.bitcast_convert_type(tb, I32).reshape(n * SLAB, LANES)


def _resident_table_spec(tab):
    return pl.BlockSpec(tab.shape, lambda i: (0, 0), pipeline_mode=pl.Buffered(1))


def _gather_rows(stage_ref, u, tab_ref, w32_ref):
    for k in range(PEER_PAIRS):
        r = pl.multiple_of(stage_ref[0, u * PEER_PAIRS + k], SLAB)
        w32_ref[k * SLAB:(k + 1) * SLAB, :] = tab_ref[pl.ds(r, SLAB), :]


def _staged_token_loop(idx_hbm, stages, sems, tab_ref, compute, bufs, tb):
    step = pl.program_id(0)
    groups_per_step = tb // GROUP
    total = pl.num_programs(0) * groups_per_step
    first = step * groups_per_step

    def copy(g, slot):
        return pltpu.make_async_copy(idx_hbm.at[g], stages[slot], sems.at[slot])

    @pl.when(step == 0)
    def _():
        for g in range(RING - 1):
            copy(g, g).start()
        copy(0, 0).wait()

    _gather_rows(stages[0], 0, tab_ref, bufs[0])

    def trip(t, carry):
        for r in range(RING):
            g = first + t * RING + r
            for u in range(GROUP):
                nxt = (r, u + 1) if u + 1 < GROUP else ((r + 1) % RING, 0)
                _gather_rows(stages[nxt[0]], nxt[1], tab_ref, bufs[(u + 1) % 2])
                if u == 0:
                    copy(g + 1, (r + 1) % RING).wait()
                    copy(g + RING - 1, (r + RING - 1) % RING).start()
                compute((t * RING + r) * GROUP + u, bufs[u % 2])
        return carry

    lax.fori_loop(0, groups_per_step // RING, trip, 0)

    @pl.when(step == pl.num_programs(0) - 1)
    def _():
        for g in range(1, RING - 1):
            copy(total + g, g % RING).wait()


def _peer_u_kernel(idx_hbm, h2_ref, g_ref, diag_ref, gsum_ref, tab_ref, w_ref, z_ref, *scratch, tb):
    stages, sems, bufs = scratch[:RING], scratch[RING], scratch[RING + 1:]
    diag = diag_ref[...]

    def compute(i, w32_ref):
        wb = pltpu.bitcast(w32_ref[...], BF16)
        hi, lo = _split_bf16(h2_ref[i])
        o = _nt(jnp.concatenate([hi, lo], axis=0), wb)
        z_ref[pl.ds(pl.multiple_of(i * ROW_TILES, ROW_TILES), ROW_TILES), :] = (o[:ROW_TILES] + o[ROW_TILES:]) * diag

    _staged_token_loop(idx_hbm, stages, sems, tab_ref, compute, bufs, tb)
    a8 = _dot2(z_ref[...], gsum_ref[...])
    a = jnp.sum(a8.reshape(tb, ROW_TILES, PEER_PAIRS), axis=1)
    gelu = 0.5 * a * (1.0 + lax.erf(a * (2.0 ** -0.5)))
    w_ref[...] = g_ref[...] * gelu


def _peer_v_kernel(idx_hbm, w_ref, x1_ref, g2_ref, diag_ref, gexp_ref, tab_ref, o_ref, wrep_ref, *scratch, tb):
    stages, sems, bufs = scratch[:RING], scratch[RING], scratch[RING + 1:]
    diag = diag_ref[...]
    wrep_ref[...] = _dot2(w_ref[...], gexp_ref[...])
    g2 = g2_ref[0]

    def compute(i, w32_ref):
        wb = pltpu.bitcast(w32_ref[...], BF16)
        a = wrep_ref[pl.ds(i, 1), :] * diag
        hi, lo = _split_bf16(a)
        o = jnp.dot(jnp.concatenate([hi, lo], axis=0), wb, preferred_element_type=F32)
        o_ref[i] = x1_ref[i] + g2 * (o[:ROW_TILES] + o[ROW_TILES:])

    _staged_token_loop(idx_hbm, stages, sems, tab_ref, compute, bufs, tb)


def _peer_consts():
    c = np.arange(PEER_PAIRS * ROW_TILES)
    diag = (c[None, :] % ROW_TILES == np.arange(ROW_TILES)[:, None]).astype(np.float32)
    gsum = (c[:, None] // ROW_TILES == np.arange(PEER_PAIRS)[None, :]).astype(np.float32)
    return jnp.asarray(diag), jnp.asarray(gsum, BF16), jnp.asarray(gsum.T, BF16)


def _peer_u(idx4, h2r, g, utab, tb):
    N = h2r.shape[0]
    diag, gsum, _ = _peer_consts()
    cst = lambda a: pl.BlockSpec(a.shape, lambda i: (0,) * a.ndim)
    return pl.pallas_call(
        functools.partial(_peer_u_kernel, tb=tb),
        grid=(N // tb,),
        in_specs=[pl.BlockSpec(memory_space=pl.ANY),
                  pl.BlockSpec((tb, ROW_TILES, LANES), lambda i: (i, 0, 0)),
                  pl.BlockSpec((tb, PEER_PAIRS), lambda i: (i, 0)),
                  cst(diag), cst(gsum), _resident_table_spec(utab)],
        out_specs=pl.BlockSpec((tb, PEER_PAIRS), lambda i: (i, 0)),
        out_shape=jax.ShapeDtypeStruct((N, PEER_PAIRS), F32),
        scratch_shapes=[pltpu.VMEM((tb * ROW_TILES, PEER_PAIRS * ROW_TILES), F32)] + _peer_scratch(),
        compiler_params=pltpu.CompilerParams(dimension_semantics=("arbitrary",),
                                             vmem_limit_bytes=VMEM_LIMIT),
        name="peer_u",
    )(idx4, h2r, g, diag, gsum, utab)


def _peer_v(idx4, w, x1r, g2r, vtab, tb, blocks_per_batch):
    N = x1r.shape[0]
    diag, _, gexp = _peer_consts()
    cst = lambda a: pl.BlockSpec(a.shape, lambda i: (0,) * a.ndim)
    return pl.pallas_call(
        functools.partial(_peer_v_kernel, tb=tb),
        grid=(N // tb,),
        in_specs=[pl.BlockSpec(memory_space=pl.ANY),
                  pl.BlockSpec((tb, PEER_PAIRS), lambda i: (i, 0)),
                  pl.BlockSpec((tb, ROW_TILES, LANES), lambda i: (i, 0, 0)),
                  pl.BlockSpec((1, ROW_TILES, LANES), lambda i: (i // blocks_per_batch, 0, 0)),
                  cst(diag), cst(gexp), _resident_table_spec(vtab)],
        out_specs=pl.BlockSpec((tb, ROW_TILES, LANES), lambda i: (i, 0, 0)),
        out_shape=jax.ShapeDtypeStruct((N, ROW_TILES, LANES), F32),
        scratch_shapes=[pltpu.VMEM((tb, PEER_PAIRS * ROW_TILES), F32)] + _peer_scratch(),
        compiler_params=pltpu.CompilerParams(dimension_semantics=("arbitrary",),
                                             vmem_limit_bytes=VMEM_LIMIT),
        name="peer_v",
    )(idx4, w, x1r, g2r, diag, gexp, vtab)


def _permute_w_in(w):
    o = np.cumsum([0, 512, 512, 512, 512, 64, 64, 512, 64, 8, 2048])
    dq, dk, dv, sq, sk, sv, iq, ik, iw, gate = [w[:, o[i]:o[i + 1]] for i in range(10)]
    pad = jnp.zeros((w.shape[0], LANES - IDX_HEADS), w.dtype)
    return jnp.concatenate([dq, dk, dv, sq, iq, sk, sk, ik, ik, sv, sv, iw, pad, gate], axis=1)


def _rope_consts(diff_q_g, diff_k_g, dsa_q_g, dsa_k_g):
    inv = ROPE_THETA ** (-(jnp.arange(0, HEAD_DIM, 2, dtype=F32) / HEAD_DIM))
    inv128 = jnp.tile(inv, LANES // HALF).reshape(1, LANES)
    sgn = np.where(np.arange(LANES) % HEAD_DIM < HALF, -1.0, 1.0).astype(np.float32).reshape(1, LANES)
    grp = np.arange(LANES) // HEAD_DIM
    gmat = jnp.asarray((grp[:, None] == grp[None, :]).astype(np.float32) / HEAD_DIM, BF16)
    t8 = lambda g: jnp.tile(g, 8).reshape(1, 512)
    return (inv128, jnp.asarray(sgn), t8(diff_q_g), t8(diff_k_g), t8(dsa_q_g),
            jnp.tile(dsa_k_g, 2).reshape(1, LANES), gmat)


def kernel(x, c, positions, w_ada, b_ada, norm1_g, w_in, diff_q_g, diff_k_g, diff_lam_q1, diff_lam_k1, diff_lam_q2, diff_lam_k2, diff_out_g, dsa_q_g, dsa_k_g, w_branch_a, w_branch_b, w_out, norm2_g, peer_w_q, peer_sub_keys, peer_u, peer_v):
    B, T, D = x.shape
    N = B * T
    assert w_ada.shape[0] == 1, "single-layer block"
    tm = min(256, T)
    tq_diff = min(256, T)
    tq_dsa = min(128, T)
    tk_dsa = min(256, T)
    tb = 128

    mod = _ada(c, w_ada[0], b_ada[0])
    sh1, sc1, g1, sh2, sc2, g2 = [m.reshape(B, 1, D) for m in jnp.split(mod, 6, axis=-1)]

    consts = _rope_consts(diff_q_g[0], diff_k_g[0], dsa_q_g[0], dsa_k_g[0])
    w_perm = _permute_w_in(w_in[0]).astype(BF16)
    dq, dk, dv, sq, iq, skk, ikk, svv, iw, gates = _inproj(
        x, sc1, sh1, norm1_g, positions.reshape(B, T, 1), consts, w_perm, tm)

    lams = [v.reshape(1, HEAD_DIM) for v in (diff_lam_q1[0], diff_lam_k1[0], diff_lam_q2[0], diff_lam_k2[0])]
    o_a = _diff_attention(dq, dk, dv, lams, diff_out_g[0], tq_diff)
    o_b = _dsa_attention(sq, iq, iw, skk, ikk, svv[..., :HEAD_DIM], tq_dsa, tk_dsa)

    sub = peer_sub_keys[0].reshape(2 * PEER_HEADS, PEER_NKEYS, LANES).astype(BF16)
    x1, h2, eidx_t, gw_t = _mix(o_a, o_b, gates, x, g1, sc2, sh2, norm2_g,
                                w_branch_a[0].astype(BF16), w_branch_b[0].astype(BF16),
                                w_out[0].astype(BF16), peer_w_q[0].astype(BF16), sub, tm)

    idx4 = (eidx_t * SLAB).reshape(PEER_PAIRS, N // GROUP, GROUP).transpose(1, 2, 0).reshape(N // GROUP, -1)
    idx4 = jnp.pad(idx4, ((0, RING - 1), (0, 0))).reshape(-1, 1, GROUP * PEER_PAIRS)
    gw = gw_t.T
    w = _peer_u(idx4, h2.reshape(N, ROW_TILES, LANES), gw, _pack_table(peer_u[0]), tb)
    out = _peer_v(idx4, w, x1.reshape(N, ROW_TILES, LANES), g2.reshape(B, ROW_TILES, LANES),
                  _pack_table(peer_v[0]), tb, T // tb)
    return out.reshape(B, T, D)
```

```python
import functools
import math

import numpy as np
import jax
import jax.numpy as jnp
from jax import lax
from jax.experimental import pallas as pl
from jax.experimental.pallas import tpu as pltpu

F32 = jnp.float32
BF16 = jnp.bfloat16
I32 = jnp.int32

LANES = 128
HEAD_DIM = 64
HALF = HEAD_DIM // 2
CHUNK = 64
ROPE_THETA = 10000.0
EPS = 1e-6
DIFF_HEADS = 4
DSA_HEADS = 8
IDX_HEADS = 8
DSA_TOPK_MAX = 256
PEER_HEADS = 8
PEER_NKEYS = 128
PEER_TOPK = 16
PEER_PAIRS = PEER_HEADS * PEER_TOPK
LAMBDA_INIT = 0.8 - 0.6 * math.exp(-0.3 * 0)
NEG = -1e30
ATT_TILE = 256
VMEM_LIMIT = 56 * 1024 * 1024

C_DQ, C_DK, C_DV, C_SQ, C_IQ = 0, 512, 1024, 1536, 2048
C_SK, C_IK, C_SV, C_IW, C_GATE = 2560, 2688, 2816, 2944, 3072
N_COLS = C_GATE + 2048


def _nt(a, b):
    return lax.dot_general(a, b, (((1,), (1,)), ((), ())), preferred_element_type=F32)


def _split_bf16(x):
    hi = x.astype(BF16)
    lo = (x - hi.astype(F32)).astype(BF16)
    return hi, lo


def _dot2(x, w):
    hi, lo = _split_bf16(x)
    return (jnp.dot(hi, w, preferred_element_type=F32)
            + jnp.dot(lo, w, preferred_element_type=F32))


def _ada_kernel(c_ref, w_ref, b_ref, o_ref):
    c = c_ref[...]
    s = c * jax.nn.sigmoid(c)
    o_ref[...] = jnp.dot(s, w_ref[...], preferred_element_type=F32,
                         precision=lax.Precision.HIGHEST) + b_ref[...]


def _ada(c, w, b):
    B, D = c.shape
    n = w.shape[1]
    bn = 1024
    return pl.pallas_call(
        _ada_kernel,
        grid=(n // bn,),
        in_specs=[pl.BlockSpec((B, D), lambda j: (0, 0)),
                  pl.BlockSpec((D, bn), lambda j: (0, j)),
                  pl.BlockSpec((1, bn), lambda j: (0, j))],
        out_specs=pl.BlockSpec((B, bn), lambda j: (0, j)),
        out_shape=jax.ShapeDtypeStruct((B, n), F32),
        compiler_params=pltpu.CompilerParams(dimension_semantics=("parallel",),
                                             vmem_limit_bytes=VMEM_LIMIT),
        name="ada",
    )(c, w, b.reshape(1, n))


def _rope128(y, cos, sin_signed, first_half):
    partner = jnp.where(first_half, pltpu.roll(y, LANES - HALF, 1), pltpu.roll(y, HALF, 1))
    return y * cos + partner * sin_signed


def _group_ms(y, gmat):
    return _dot2(y * y, gmat)


def _inproj_kernel(x_ref, sc_ref, sh_ref, n1_ref, pos_ref, inv_ref, sgn_ref, gq_ref, gk_ref,
                   gsq_ref, gsk_ref, gmat_ref, w_ref,
                   dq_ref, dk_ref, dv_ref, sq_ref, iq_ref, sk_ref, ik_ref, sv_ref, iw_ref, gate_ref):
    x = x_ref[0]
    ms = jnp.mean(x * x, axis=-1, keepdims=True)
    h = x * lax.rsqrt(ms + EPS) * n1_ref[...] * (1.0 + sc_ref[0]) + sh_ref[0]
    hb = h.astype(BF16)
    tm = x.shape[0]

    ang = pos_ref[0].astype(F32) * inv_ref[...]
    cos = jnp.cos(ang)
    sin = jnp.sin(ang) * sgn_ref[...]
    lane = lax.broadcasted_iota(I32, (tm, LANES), 1)
    first_half = (lane % HEAD_DIM) < HALF
    gmat = gmat_ref[...]

    def proj(c0, n):
        return jnp.dot(hb, w_ref[:, c0:c0 + n], preferred_element_type=F32)

    def normed_rope(c0, nblk, gain_ref, out_ref, scale):
        for i in range(nblk):
            y = proj(c0 + i * LANES, LANES)
            if gain_ref is not None:
                y = y * lax.rsqrt(_group_ms(y, gmat) + EPS) * gain_ref[:, i * LANES:(i + 1) * LANES]
            y = _rope128(y, cos, sin, first_half)
            if scale != 1.0:
                y = y * scale
            out_ref[0, :, i * LANES:(i + 1) * LANES] = y.astype(out_ref.dtype)

    qscale = HEAD_DIM ** -0.5
    normed_rope(C_DQ, 4, gq_ref, dq_ref, qscale)
    normed_rope(C_DK, 4, gk_ref, dk_ref, 1.0)
    dv = proj(C_DV, 512)
    sv = proj(C_SV, LANES)
    for s in range(tm // ATT_TILE):
        rows = slice(s * ATT_TILE, (s + 1) * ATT_TILE)
        for hd in range(DIFF_HEADS):
            dv_ref[0, hd, s] = dv[rows, hd * LANES:(hd + 1) * LANES].T.astype(dv_ref.dtype)
        sv_ref[0, s] = sv[rows, :].T[:HEAD_DIM].astype(sv_ref.dtype)
    normed_rope(C_SQ, 4, gsq_ref, sq_ref, qscale)
    normed_rope(C_IQ, 4, None, iq_ref, 1.0)
    normed_rope(C_SK, 1, gsk_ref, sk_ref, 1.0)
    normed_rope(C_IK, 1, None, ik_ref, 1.0)
    iw_ref[0] = proj(C_IW, LANES)
    for i in range(4):
        g = proj(C_GATE + i * 512, 512)
        gate_ref[0, :, i * 512:(i + 1) * 512] = jax.nn.sigmoid(g).astype(gate_ref.dtype)


def _inproj(x, sc1, sh1, n1g, pos, consts, w_perm, tm):
    B, T, D = x.shape
    inv128, sgn128, gq, gk, gsq, gsk, gmat = consts
    tok = lambda n, dt: jax.ShapeDtypeStruct((B, T, n), dt)
    blk = lambda n: pl.BlockSpec((1, tm, n), lambda b, i: (b, i, 0))
    cst = lambda a: pl.BlockSpec(a.shape, lambda b, i: (0,) * a.ndim)
    mod = pl.BlockSpec((1, 1, D), lambda b, i: (b, 0, 0))
    n_att, per_step = T // ATT_TILE, tm // ATT_TILE
    dvt_blk = pl.BlockSpec((1, DIFF_HEADS, per_step, LANES, ATT_TILE), lambda b, i: (b, 0, i, 0, 0))
    svt_blk = pl.BlockSpec((1, per_step, HEAD_DIM, ATT_TILE), lambda b, i: (b, i, 0, 0))
    return pl.pallas_call(
        _inproj_kernel,
        grid=(B, T // tm),
        in_specs=[blk(D), mod, mod, cst(n1g), blk(1), cst(inv128), cst(sgn128), cst(gq), cst(gk),
                  cst(gsq), cst(gsk), cst(gmat), cst(w_perm)],
        out_specs=[blk(512), blk(512), dvt_blk, blk(512), blk(512), blk(128), blk(128), svt_blk,
                   blk(128), blk(2048)],
        out_shape=[tok(512, BF16), tok(512, BF16),
                   jax.ShapeDtypeStruct((B, DIFF_HEADS, n_att, LANES, ATT_TILE), BF16),
                   tok(512, BF16), tok(512, BF16), tok(128, BF16), tok(128, BF16),
                   jax.ShapeDtypeStruct((B, n_att, HEAD_DIM, ATT_TILE), BF16),
                   tok(128, F32), tok(2048, BF16)],
        compiler_params=pltpu.CompilerParams(dimension_semantics=("parallel", "parallel"),
                                             vmem_limit_bytes=VMEM_LIMIT),
        name="inproj",
    )(x, sc1, sh1, n1g, pos, inv128, sgn128, gq, gk, gsq, gsk, gmat, w_perm)


def _diff_kernel(q_ref, k_ref, vt_ref, lq1_ref, lk1_ref, lq2_ref, lk2_ref, og_ref, o_ref, *, tq):
    qi = pl.program_id(1)
    lane = lax.broadcasted_iota(I32, (tq, LANES), 1)
    q2 = []
    for h in range(DIFF_HEADS):
        q = q_ref[0, :, h * LANES:(h + 1) * LANES]
        zero = jnp.zeros_like(q)
        q2.append(jnp.concatenate([jnp.where(lane < HEAD_DIM, q, zero),
                                   jnp.where(lane >= HEAD_DIM, q, zero)], axis=0))

    def tile(j, carry, on_diagonal):
        out = []
        for h in range(DIFF_HEADS):
            m, l, acc = carry[h]
            s = _nt(k_ref[0, j, :, h * LANES:(h + 1) * LANES], q2[h])
            if on_diagonal:
                krow = lax.broadcasted_iota(I32, (tq, 2 * tq), 0)
                qcol = lax.broadcasted_iota(I32, (tq, 2 * tq), 1)
                qcol = jnp.where(qcol >= tq, qcol - tq, qcol)
                s = jnp.where(krow // CHUNK <= qcol // CHUNK, s, NEG)
            m_new = jnp.maximum(m, jnp.max(s, axis=0, keepdims=True))
            alpha = jnp.exp(m - m_new)
            p = jnp.exp(s - m_new)
            l = alpha * l + jnp.sum(p, axis=0, keepdims=True)
            acc = alpha * acc + jnp.dot(vt_ref[0, h, j], p.astype(BF16), preferred_element_type=F32)
            out.append((m_new, l, acc))
        return tuple(out)

    init = tuple((jnp.full((1, 2 * tq), NEG, F32), jnp.zeros((1, 2 * tq), F32),
                  jnp.zeros((LANES, 2 * tq), F32)) for _ in range(DIFF_HEADS))
    carry = lax.fori_loop(0, qi, lambda j, c: tile(j, c, False), init)
    res = tile(qi, carry, True)
    lam = (jnp.exp(jnp.sum(lq1_ref[...] * lk1_ref[...], axis=-1, keepdims=True))
           - jnp.exp(jnp.sum(lq2_ref[...] * lk2_ref[...], axis=-1, keepdims=True)) + LAMBDA_INIT)
    for h in range(DIFF_HEADS):
        _, l, acc = res[h]
        o = acc * (1.0 / l)
        o = o[:, :tq] - lam * o[:, tq:]
        ms = jnp.mean(o * o, axis=0, keepdims=True)
        o = o * lax.rsqrt(ms + EPS) * og_ref[...] * (1.0 - LAMBDA_INIT)
        o_ref[0, :, h * LANES:(h + 1) * LANES] = o.T.astype(o_ref.dtype)


def _diff_attention(dq, dk, vt_tiles, lams, out_g):
    B, T, _ = dq.shape
    tq = ATT_TILE
    n = T // tq
    k_tiles = dk.reshape(B, n, tq, DIFF_HEADS * LANES)
    og_cols = jnp.broadcast_to(out_g.reshape(LANES, 1), (LANES, tq))
    width = DIFF_HEADS * LANES
    cst = lambda a: pl.BlockSpec(a.shape, lambda b, i: (0,) * a.ndim)
    return pl.pallas_call(
        functools.partial(_diff_kernel, tq=tq),
        grid=(B, n),
        in_specs=[pl.BlockSpec((1, tq, width), lambda b, i: (b, i, 0)),
                  pl.BlockSpec((1, n, tq, width), lambda b, i: (b, 0, 0, 0)),
                  pl.BlockSpec((1, DIFF_HEADS, n, LANES, tq), lambda b, i: (b, 0, 0, 0, 0)),
                  cst(lams[0]), cst(lams[1]), cst(lams[2]), cst(lams[3]), cst(og_cols)],
        out_specs=pl.BlockSpec((1, tq, width), lambda b, i: (b, i, 0)),
        out_shape=jax.ShapeDtypeStruct((B, T, width), BF16),
        compiler_params=pltpu.CompilerParams(
            dimension_semantics=("parallel", "parallel"), vmem_limit_bytes=VMEM_LIMIT),
        name="diff_attention",
    )(dq, k_tiles, vt_tiles, *lams, og_cols)


def _dsa_kernel(sq_ref, iq_ref, iw_ref, sk_ref, ik_ref, svt_ref, o_ref, key_ref, *, tq, tk, topk, t_bits):
    qi = pl.program_id(1)
    n_kv = (qi * tq + tq + tk - 1) // tk
    lane = lax.broadcasted_iota(I32, (tq, LANES), 1)
    lo_half = lane < HEAD_DIM
    krow = lax.broadcasted_iota(I32, (tk, tq), 0)
    q_chunk = (qi * tq + lax.broadcasted_iota(I32, (tk, tq), 1)) // CHUNK
    idx_scale = (IDX_HEADS ** -0.5) * (HEAD_DIM ** -0.5)
    key_neg_inf = jnp.int32(-2139095041)

    def all_heads(ref):
        out = []
        for p in range(4):
            slab = ref[0, :, p * LANES:(p + 1) * LANES]
            zero = jnp.zeros_like(slab)
            out.append(jnp.where(lo_half, slab, zero))
            out.append(jnp.where(lo_half, zero, slab))
        return jnp.concatenate(out, axis=0)

    def per_head(x, h):
        return x[:, h * tq:(h + 1) * tq]

    iq_all = all_heads(iq_ref)
    iw_t = iw_ref[0].T
    w_all = jnp.concatenate([iw_t[h:h + 1, :] for h in range(IDX_HEADS)], axis=1)

    def score_body(j, carry):
        r = jnp.maximum(_nt(ik_ref[0, j], iq_all), 0.0) * w_all
        score = per_head(r, 0)
        for h in range(1, IDX_HEADS):
            score = score + per_head(r, h)
        score = score * idx_scale + 0.0
        score = jnp.where((j * tk + krow) // CHUNK <= q_chunk, score, -jnp.inf)
        bits = lax.bitcast_convert_type(score, I32)
        key_ref[j] = bits ^ ((bits >> 31) & jnp.int32(0x7FFFFFFF))
        return carry

    lax.fori_loop(0, n_kv, score_body, 0)

    def count(pred_fn):
        def body(j, c):
            hit = jnp.where(pred_fn(key_ref[j], j * tk + krow), 1.0, 0.0)
            parts = [hit[8 * i:8 * i + 8] for i in range(tk // 8)]
            while len(parts) > 1:
                parts = [parts[i] + parts[i + 1] for i in range(0, len(parts), 2)]
            return c + parts[0]
        c = lax.fori_loop(0, n_kv, body, jnp.zeros((8, tq), F32))
        return jnp.sum(c, axis=0, keepdims=True)

    def bit_body(it, t):
        cand = t + jnp.left_shift(jnp.int32(1), 31 - it)
        c = count(lambda kj, _: kj >= cand)
        return jnp.where(c >= float(topk), cand, t)

    thr = lax.fori_loop(0, 32, bit_body, jnp.full((1, tq), jnp.iinfo(jnp.int32).min, I32))
    need = float(topk) - count(lambda kj, _: kj > thr)

    def tie_body(it, jmax):
        cand = jmax + jnp.left_shift(jnp.int32(1), t_bits - 1 - it)
        c = count(lambda kj, kidx: (kj == thr) & (kidx < cand))
        return jnp.where(c < need, cand, jmax)

    ties = count(lambda kj, _: kj == thr)
    some_partial = jnp.max(jnp.where(need < ties, 1.0, 0.0)) > 0.0
    jmax = lax.cond(some_partial,
                    lambda: lax.fori_loop(0, t_bits, tie_body, jnp.zeros((1, tq), I32)),
                    lambda: jnp.full((1, tq), (1 << t_bits) - 1, I32))

    q_all = all_heads(sq_ref)
    width = DSA_HEADS * tq

    def attn_body(j, carry):
        m, l, acc = carry
        kj = key_ref[j]
        sel = ((kj > thr) | ((kj == thr) & (j * tk + krow <= jmax))) & (kj != key_neg_inf)
        sel1 = jnp.where(sel, 1.0, 0.0)
        sel_all = jnp.concatenate([sel1] * DSA_HEADS, axis=1)
        s = jnp.where(sel_all > 0.5, _nt(sk_ref[0, j], q_all), NEG)
        m_new = jnp.maximum(m, jnp.max(s, axis=0, keepdims=True))
        alpha = jnp.exp(m - m_new)
        p = jnp.exp(s - m_new) * sel_all
        l = alpha * l + jnp.sum(p, axis=0, keepdims=True)
        acc = alpha * acc + jnp.dot(svt_ref[0, j], p.astype(BF16), preferred_element_type=F32)
        return m_new, l, acc

    init = (jnp.full((1, width), NEG, F32), jnp.zeros((1, width), F32), jnp.zeros((HEAD_DIM, width), F32))
    _, l, acc = lax.fori_loop(0, n_kv, attn_body, init)
    o_all = acc * (1.0 / l)
    o_t = jnp.concatenate([per_head(o_all, h) for h in range(DSA_HEADS)], axis=0)
    o_ref[0] = o_t.T.astype(o_ref.dtype)


def _dsa_attention(sq, iq, iw, skk, ikk, svt_tiles, tq):
    B, T, _ = sq.shape
    tk = ATT_TILE
    n = T // tk
    topk = min(DSA_TOPK_MAX, T // 4)
    t_bits = max(1, (T - 1).bit_length())
    sk_tiles = skk.reshape(B, n, tk, LANES)
    ik_tiles = ikk.reshape(B, n, tk, LANES)
    qblk = lambda c: pl.BlockSpec((1, tq, c), lambda b, i: (b, i, 0))
    kblk = lambda a: pl.BlockSpec((1,) + a.shape[1:], lambda b, i: (b, 0, 0, 0))
    return pl.pallas_call(
        functools.partial(_dsa_kernel, tq=tq, tk=tk, topk=topk, t_bits=t_bits),
        grid=(B, T // tq),
        in_specs=[qblk(512), qblk(512), qblk(LANES), kblk(sk_tiles), kblk(ik_tiles), kblk(svt_tiles)],
        out_specs=qblk(512),
        out_shape=jax.ShapeDtypeStruct((B, T, DSA_HEADS * HEAD_DIM), BF16),
        scratch_shapes=[pltpu.VMEM((n, tk, tq), I32)],
        compiler_params=pltpu.CompilerParams(dimension_semantics=("parallel", "parallel"),
                                             vmem_limit_bytes=VMEM_LIMIT),
        name="dsa_attention",
    )(sq, iq, iw, sk_tiles, ik_tiles, svt_tiles)


def _top_rows(s, rid, k, payload=None):
    big = float(2 ** 20)
    vals, ids, pay = [], [], []
    for _ in range(k):
        m = jnp.max(s, axis=0, keepdims=True)
        am = jnp.min(jnp.where(s == m, rid, big), axis=0, keepdims=True)
        hit = rid == am
        vals.append(m)
        ids.append(am)
        if payload is not None:
            pay.append(jnp.sum(jnp.where(hit, payload, 0.0), axis=0, keepdims=True))
        s = jnp.where(hit, -jnp.inf, s)
    cat = lambda xs: jnp.concatenate(xs, axis=0)
    return cat(vals), cat(ids), (cat(pay) if payload is not None else None)


def _pair_groups():
    groups = [(0, 0), (0, 8)] + [(a, 0) for a in range(1, 8)]
    return groups, [PEER_TOPK // (a + 1) for a, _ in groups]


def _pair_candidates(v1, i1, v2, i2):
    n = v1.shape[1]
    brow = lax.broadcasted_iota(I32, (8, n), 0)
    groups, limits = _pair_groups()
    cand, cidx, pos = [], [], []
    for (a, b0), lim in zip(groups, limits):
        ok = brow + b0 < lim
        cand.append(jnp.where(ok, v1[a:a + 1] + v2[b0:b0 + 8], -jnp.inf))
        cidx.append(i1[a:a + 1] * float(PEER_NKEYS) + i2[b0:b0 + 8])
        pos.append((brow + (a * PEER_TOPK + b0)).astype(F32))
    cand.append(v1[8:16] + v2[0:1])
    cidx.append(i1[8:16] * float(PEER_NKEYS) + i2[0:1])
    pos.append(((brow + 8) * PEER_TOPK).astype(F32))
    cat = lambda xs: jnp.concatenate(xs, axis=0)
    return cat(cand), cat(cidx), cat(pos)


def _mix_kernel(oa_ref, ob_ref, gate_ref, x_ref, g1_ref, sc_ref, sh_ref, n2_ref, wa_ref, wb_ref, wo_ref,
                wq_ref, sub_ref, x1_ref, h2_ref, eidx_ref, gw_ref, q_ref):
    ya = jnp.dot(oa_ref[0], wa_ref[...], preferred_element_type=F32)
    yb = jnp.dot(ob_ref[0], wb_ref[...], preferred_element_type=F32)
    D = ya.shape[1]
    tm = ya.shape[0]
    merged = gate_ref[0, :, :D].astype(F32) * ya + gate_ref[0, :, D:].astype(F32) * yb
    x1 = x_ref[0] + g1_ref[0] * jnp.dot(merged.astype(BF16), wo_ref[...], preferred_element_type=F32)
    x1_ref[0] = x1
    ms = jnp.mean(x1 * x1, axis=-1, keepdims=True)
    h2 = x1 * lax.rsqrt(ms + EPS) * n2_ref[...] * (1.0 + sc_ref[0]) + sh_ref[0]
    h2_ref[0] = h2
    q_ref[...] = jnp.dot(h2.astype(BF16), wq_ref[...], preferred_element_type=F32).astype(BF16)

    key_id = lax.broadcasted_iota(I32, (PEER_NKEYS, LANES), 0).astype(F32)
    for t in range(tm // LANES):
        cols = slice(t * LANES, (t + 1) * LANES)
        experts, gates_t = [], []
        for h in range(PEER_HEADS):
            tops = []
            for p in range(2):
                hp = 2 * h + p
                s_t = _nt(sub_ref[hp], q_ref[cols, hp * LANES:(hp + 1) * LANES])
                v, i, _ = _top_rows(s_t, key_id, PEER_TOPK)
                tops.append((v, i))
            cand, cidx, pos = _pair_candidates(*tops[0], *tops[1])
            sc, _, e = _top_rows(cand, pos, PEER_TOPK, payload=cidx)
            ex = jnp.exp(sc - sc[0:1])
            g = ex / jnp.sum(ex, axis=0, keepdims=True)
            experts.append(e)
            gates_t.append(g)
        eidx_ref[cols, :] = (jnp.concatenate(experts, axis=0).T * float(SLAB)).astype(I32)
        gw_ref[cols, :] = jnp.concatenate(gates_t, axis=0).T


def _mix(o_a, o_b, gates, x, g1, sc2, sh2, n2g, wa, wb, wo, wq, sub, tm):
    B, T, D = x.shape
    blk = lambda n: pl.BlockSpec((1, tm, n), lambda b, i: (b, i, 0))
    cst = lambda a: pl.BlockSpec(a.shape, lambda b, i: (0,) * a.ndim)
    mod = pl.BlockSpec((1, 1, D), lambda b, i: (b, 0, 0))
    nblk = T // tm
    tblk = pl.BlockSpec((tm, PEER_PAIRS), lambda b, i: (b * nblk + i, 0))
    return pl.pallas_call(
        _mix_kernel,
        grid=(B, nblk),
        in_specs=[blk(512), blk(512), blk(2 * D), blk(D), mod, mod, mod, cst(n2g), cst(wa), cst(wb),
                  cst(wo), cst(wq), cst(sub)],
        out_specs=[blk(D), blk(D), tblk, tblk],
        out_shape=[jax.ShapeDtypeStruct((B, T, D), F32), jax.ShapeDtypeStruct((B, T, D), F32),
                   jax.ShapeDtypeStruct((B * T, PEER_PAIRS), I32),
                   jax.ShapeDtypeStruct((B * T, PEER_PAIRS), F32)],
        scratch_shapes=[pltpu.VMEM((tm, wq.shape[1]), BF16)],
        compiler_params=pltpu.CompilerParams(dimension_semantics=("parallel", "parallel"),
                                             vmem_limit_bytes=VMEM_LIMIT),
        name="mix_route",
    )(o_a, o_b, gates, x, g1, sc2, sh2, n2g, wa, wb, wo, wq, sub)


SLAB = 4
ROW_TILES = 8
GROUP = 8
RING = 4


def _peer_scratch():
    return ([pltpu.SMEM((1, GROUP * PEER_PAIRS), I32) for _ in range(RING)] + [pltpu.SemaphoreType.DMA((RING,))]
            + [pltpu.VMEM((PEER_PAIRS * SLAB, LANES), I32) for _ in range(2)])


def _pack_table(t):
    n, d = t.shape
    tb = t.astype(BF16).reshape(n, SLAB, 2, LANES).transpose(0, 1, 3, 2)
    return lax.bitcast_convert_type(tb, I32).reshape(n * SLAB, LANES)


def _resident_table_spec(tab):
    return pl.BlockSpec(tab.shape, lambda i: (0, 0), pipeline_mode=pl.Buffered(1))


def _gather_rows(stage_ref, u, tab_ref, w32_ref):
    for k in range(PEER_PAIRS):
        r = pl.multiple_of(stage_ref[0, u * PEER_PAIRS + k], SLAB)
        w32_ref[k * SLAB:(k + 1) * SLAB, :] = tab_ref[pl.ds(r, SLAB), :]


def _staged_token_loop(idx_hbm, stages, sems, tab_ref, compute, bufs, tb):
    step = pl.program_id(0)
    groups_per_step = tb // GROUP
    total = pl.num_programs(0) * groups_per_step
    first = step * groups_per_step

    def copy(g, slot):
        return pltpu.make_async_copy(idx_hbm.at[g], stages[slot], sems.at[slot])

    @pl.when(step == 0)
    def _():
        for g in range(RING - 1):
            copy(g, g).start()
        copy(0, 0).wait()

    _gather_rows(stages[0], 0, tab_ref, bufs[0])

    def trip(t, carry):
        for r in range(RING):
            g = first + t * RING + r
            for u in range(GROUP):
                nxt = (r, u + 1) if u + 1 < GROUP else ((r + 1) % RING, 0)
                _gather_rows(stages[nxt[0]], nxt[1], tab_ref, bufs[(u + 1) % 2])
                if u == 0:
                    copy(g + 1, (r + 1) % RING).wait()
                    copy(g + RING - 1, (r + RING - 1) % RING).start()
                compute((t * RING + r) * GROUP + u, bufs[u % 2])
        return carry

    lax.fori_loop(0, groups_per_step // RING, trip, 0)

    @pl.when(step == pl.num_programs(0) - 1)
    def _():
        for g in range(1, RING - 1):
            copy(total + g, g % RING).wait()


def _peer_u_kernel(idx_hbm, h2_ref, g_ref, diag_ref, gsum_ref, tab_ref, w_ref, z_ref, *scratch, tb):
    stages, sems, bufs = scratch[:RING], scratch[RING], scratch[RING + 1:]
    diag = diag_ref[...]

    def compute(i, w32_ref):
        wb = pltpu.bitcast(w32_ref[...], BF16)
        hi, lo = _split_bf16(h2_ref[i])
        o = _nt(jnp.concatenate([hi, lo], axis=0), wb)
        z_ref[pl.ds(pl.multiple_of(i * ROW_TILES, ROW_TILES), ROW_TILES), :] = (o[:ROW_TILES] + o[ROW_TILES:]) * diag

    _staged_token_loop(idx_hbm, stages, sems, tab_ref, compute, bufs, tb)
    a8 = _dot2(z_ref[...], gsum_ref[...])
    a = jnp.sum(a8.reshape(tb, ROW_TILES, PEER_PAIRS), axis=1)
    gelu = 0.5 * a * (1.0 + lax.erf(a * (2.0 ** -0.5)))
    w_ref[...] = g_ref[...] * gelu


def _peer_v_kernel(idx_hbm, w_ref, x1_ref, g2_ref, diag_ref, gexp_ref, tab_ref, o_ref, wrep_ref, *scratch, tb):
    stages, sems, bufs = scratch[:RING], scratch[RING], scratch[RING + 1:]
    diag = diag_ref[...]
    wrep_ref[...] = _dot2(w_ref[...], gexp_ref[...])
    g2 = g2_ref[0]

    def compute(i, w32_ref):
        wb = pltpu.bitcast(w32_ref[...], BF16)
        a = wrep_ref[pl.ds(i, 1), :] * diag
        hi, lo = _split_bf16(a)
        o = jnp.dot(jnp.concatenate([hi, lo], axis=0), wb, preferred_element_type=F32)
        o_ref[i] = x1_ref[i] + g2 * (o[:ROW_TILES] + o[ROW_TILES:])

    _staged_token_loop(idx_hbm, stages, sems, tab_ref, compute, bufs, tb)


def _peer_consts():
    c = np.arange(PEER_PAIRS * ROW_TILES)
    diag = (c[None, :] % ROW_TILES == np.arange(ROW_TILES)[:, None]).astype(np.float32)
    gsum = (c[:, None] // ROW_TILES == np.arange(PEER_PAIRS)[None, :]).astype(np.float32)
    return jnp.asarray(diag), jnp.asarray(gsum, BF16), jnp.asarray(gsum.T, BF16)


def _peer_u(idx4, h2r, g, utab, tb):
    N = h2r.shape[0]
    diag, gsum, _ = _peer_consts()
    cst = lambda a: pl.BlockSpec(a.shape, lambda i: (0,) * a.ndim)
    return pl.pallas_call(
        functools.partial(_peer_u_kernel, tb=tb),
        grid=(N // tb,),
        in_specs=[pl.BlockSpec(memory_space=pl.ANY),
                  pl.BlockSpec((tb, ROW_TILES, LANES), lambda i: (i, 0, 0)),
                  pl.BlockSpec((tb, PEER_PAIRS), lambda i: (i, 0)),
                  cst(diag), cst(gsum), _resident_table_spec(utab)],
        out_specs=pl.BlockSpec((tb, PEER_PAIRS), lambda i: (i, 0)),
        out_shape=jax.ShapeDtypeStruct((N, PEER_PAIRS), F32),
        scratch_shapes=[pltpu.VMEM((tb * ROW_TILES, PEER_PAIRS * ROW_TILES), F32)] + _peer_scratch(),
        compiler_params=pltpu.CompilerParams(dimension_semantics=("arbitrary",),
                                             vmem_limit_bytes=VMEM_LIMIT),
        name="peer_u",
    )(idx4, h2r, g, diag, gsum, utab)


def _peer_v(idx4, w, x1r, g2r, vtab, tb, blocks_per_batch):
    N = x1r.shape[0]
    diag, _, gexp = _peer_consts()
    cst = lambda a: pl.BlockSpec(a.shape, lambda i: (0,) * a.ndim)
    return pl.pallas_call(
        functools.partial(_peer_v_kernel, tb=tb),
        grid=(N // tb,),
        in_specs=[pl.BlockSpec(memory_space=pl.ANY),
                  pl.BlockSpec((tb, PEER_PAIRS), lambda i: (i, 0)),
                  pl.BlockSpec((tb, ROW_TILES, LANES), lambda i: (i, 0, 0)),
                  pl.BlockSpec((1, ROW_TILES, LANES), lambda i: (i // blocks_per_batch, 0, 0)),
                  cst(diag), cst(gexp), _resident_table_spec(vtab)],
        out_specs=pl.BlockSpec((tb, ROW_TILES, LANES), lambda i: (i, 0, 0)),
        out_shape=jax.ShapeDtypeStruct((N, ROW_TILES, LANES), F32),
        scratch_shapes=[pltpu.VMEM((tb, PEER_PAIRS * ROW_TILES), F32)] + _peer_scratch(),
        compiler_params=pltpu.CompilerParams(dimension_semantics=("arbitrary",),
                                             vmem_limit_bytes=VMEM_LIMIT),
        name="peer_v",
    )(idx4, w, x1r, g2r, diag, gexp, vtab)


def _permute_w_in(w):
    o = np.cumsum([0, 512, 512, 512, 512, 64, 64, 512, 64, 8, 2048])
    dq, dk, dv, sq, sk, sv, iq, ik, iw, gate = [w[:, o[i]:o[i + 1]] for i in range(10)]
    pad = jnp.zeros((w.shape[0], LANES - IDX_HEADS), w.dtype)
    return jnp.concatenate([dq, dk, dv, sq, iq, sk, sk, ik, ik, sv, sv, iw, pad, gate], axis=1)


def _rope_consts(diff_q_g, diff_k_g, dsa_q_g, dsa_k_g):
    inv = ROPE_THETA ** (-(jnp.arange(0, HEAD_DIM, 2, dtype=F32) / HEAD_DIM))
    inv128 = jnp.tile(inv, LANES // HALF).reshape(1, LANES)
    sgn = np.where(np.arange(LANES) % HEAD_DIM < HALF, -1.0, 1.0).astype(np.float32).reshape(1, LANES)
    grp = np.arange(LANES) // HEAD_DIM
    gmat = jnp.asarray((grp[:, None] == grp[None, :]).astype(np.float32) / HEAD_DIM, BF16)
    t8 = lambda g: jnp.tile(g, 8).reshape(1, 512)
    return (inv128, jnp.asarray(sgn), t8(diff_q_g), t8(diff_k_g), t8(dsa_q_g),
            jnp.tile(dsa_k_g, 2).reshape(1, LANES), gmat)


def kernel(x, c, positions, w_ada, b_ada, norm1_g, w_in, diff_q_g, diff_k_g, diff_lam_q1, diff_lam_k1, diff_lam_q2, diff_lam_k2, diff_out_g, dsa_q_g, dsa_k_g, w_branch_a, w_branch_b, w_out, norm2_g, peer_w_q, peer_sub_keys, peer_u, peer_v):
    B, T, D = x.shape
    N = B * T
    assert w_ada.shape[0] == 1, "single-layer block"
    assert T % ATT_TILE == 0, "sequence length must be a multiple of the attention tile"
    tm_in = min(512, T)
    tm_mix = ATT_TILE
    tq_dsa = 128
    tb = 128

    mod = _ada(c, w_ada[0], b_ada[0])
    sh1, sc1, g1, sh2, sc2, g2 = [m.reshape(B, 1, D) for m in jnp.split(mod, 6, axis=-1)]

    consts = _rope_consts(diff_q_g[0], diff_k_g[0], dsa_q_g[0], dsa_k_g[0])
    w_perm = _permute_w_in(w_in[0]).astype(BF16)
    dq, dk, dvt, sq, iq, skk, ikk, svt, iw, gates = _inproj(
        x, sc1, sh1, norm1_g, positions.reshape(B, T, 1), consts, w_perm, tm_in)

    lams = [v.reshape(1, HEAD_DIM) for v in (diff_lam_q1[0], diff_lam_k1[0], diff_lam_q2[0], diff_lam_k2[0])]
    o_a = _diff_attention(dq, dk, dvt, lams, diff_out_g[0])
    o_b = _dsa_attention(sq, iq, iw, skk, ikk, svt, tq_dsa)

    sub = peer_sub_keys[0].reshape(2 * PEER_HEADS, PEER_NKEYS, LANES).astype(BF16)
    x1, h2, eidx, gw = _mix(o_a, o_b, gates, x, g1, sc2, sh2, norm2_g,
                            w_branch_a[0].astype(BF16), w_branch_b[0].astype(BF16),
                            w_out[0].astype(BF16), peer_w_q[0].astype(BF16), sub, tm_mix)

    idx4 = jnp.pad(eidx.reshape(N // GROUP, GROUP * PEER_PAIRS), ((0, RING - 1), (0, 0)))
    idx4 = idx4.reshape(-1, 1, GROUP * PEER_PAIRS)
    w = _peer_u(idx4, h2.reshape(N, ROW_TILES, LANES), gw, _pack_table(peer_u[0]), tb)
    out = _peer_v(idx4, w, x1.reshape(N, ROW_TILES, LANES), g2.reshape(B, ROW_TILES, LANES),
                  _pack_table(peer_v[0]), tb, T // tb)
    return out.reshape(B, T, D)
```

```python
import functools
import math

import numpy as np
import jax
import jax.numpy as jnp
from jax import lax
from jax.experimental import pallas as pl
from jax.experimental.pallas import tpu as pltpu

F32 = jnp.float32
BF16 = jnp.bfloat16
I32 = jnp.int32

LANES = 128
HEAD_DIM = 64
HALF = HEAD_DIM // 2
CHUNK = 64
ROPE_THETA = 10000.0
EPS = 1e-6
DIFF_HEADS = 4
DSA_HEADS = 8
IDX_HEADS = 8
DSA_TOPK_MAX = 256
PEER_HEADS = 8
PEER_NKEYS = 128
PEER_TOPK = 16
PEER_PAIRS = PEER_HEADS * PEER_TOPK
LAMBDA_INIT = 0.8 - 0.6 * math.exp(-0.3 * 0)
NEG = -1e30
ATT_TILE = 256
VMEM_LIMIT = 56 * 1024 * 1024

C_DQ, C_DK, C_DV, C_SQ, C_IQ = 0, 512, 1024, 1536, 2048
C_SK, C_IK, C_SV, C_IW, C_GATE = 2560, 2688, 2816, 2944, 3072
N_COLS = C_GATE + 2048


def _nt(a, b):
    return lax.dot_general(a, b, (((1,), (1,)), ((), ())), preferred_element_type=F32)


def _split_bf16(x):
    hi = x.astype(BF16)
    lo = (x - hi.astype(F32)).astype(BF16)
    return hi, lo


def _dot2(x, w):
    hi, lo = _split_bf16(x)
    return (jnp.dot(hi, w, preferred_element_type=F32)
            + jnp.dot(lo, w, preferred_element_type=F32))


def _ada_kernel(c_ref, w_ref, b_ref, o_ref):
    c = c_ref[...]
    s = c * jax.nn.sigmoid(c)
    o_ref[...] = jnp.dot(s, w_ref[...], preferred_element_type=F32,
                         precision=lax.Precision.HIGHEST) + b_ref[...]


def _ada(c, w, b):
    B, D = c.shape
    n = w.shape[1]
    bn = 1024
    return pl.pallas_call(
        _ada_kernel,
        grid=(n // bn,),
        in_specs=[pl.BlockSpec((B, D), lambda j: (0, 0)),
                  pl.BlockSpec((D, bn), lambda j: (0, j)),
                  pl.BlockSpec((1, bn), lambda j: (0, j))],
        out_specs=pl.BlockSpec((B, bn), lambda j: (0, j)),
        out_shape=jax.ShapeDtypeStruct((B, n), F32),
        compiler_params=pltpu.CompilerParams(dimension_semantics=("parallel",),
                                             vmem_limit_bytes=VMEM_LIMIT),
        name="ada",
    )(c, w, b.reshape(1, n))


def _rope128(y, cos, sin_signed, first_half):
    partner = jnp.where(first_half, pltpu.roll(y, LANES - HALF, 1), pltpu.roll(y, HALF, 1))
    return y * cos + partner * sin_signed


def _group_ms(y, gmat):
    return _dot2(y * y, gmat)


def _inproj_kernel(x_ref, sc_ref, sh_ref, n1_ref, pos_ref, inv_ref, sgn_ref, gq_ref, gk_ref,
                   gsq_ref, gsk_ref, gmat_ref, w_ref,
                   dq_ref, dk_ref, dv_ref, sq_ref, iq_ref, sk_ref, ik_ref, sv_ref, iw_ref, gate_ref):
    x = x_ref[0]
    ms = jnp.mean(x * x, axis=-1, keepdims=True)
    h = x * lax.rsqrt(ms + EPS) * n1_ref[...] * (1.0 + sc_ref[0]) + sh_ref[0]
    hb = h.astype(BF16)
    tm = x.shape[0]

    ang = pos_ref[0].astype(F32) * inv_ref[...]
    cos = jnp.cos(ang)
    sin = jnp.sin(ang) * sgn_ref[...]
    lane = lax.broadcasted_iota(I32, (tm, LANES), 1)
    first_half = (lane % HEAD_DIM) < HALF
    gmat = gmat_ref[...]

    def proj(c0, n):
        return jnp.dot(hb, w_ref[:, c0:c0 + n], preferred_element_type=F32)

    def normed_rope(c0, nblk, gain_ref, out_ref, scale):
        for i in range(nblk):
            y = proj(c0 + i * LANES, LANES)
            if gain_ref is not None:
                y = y * lax.rsqrt(_group_ms(y, gmat) + EPS) * gain_ref[:, i * LANES:(i + 1) * LANES]
            y = _rope128(y, cos, sin, first_half)
            if scale != 1.0:
                y = y * scale
            out_ref[0, :, i * LANES:(i + 1) * LANES] = y.astype(out_ref.dtype)

    qscale = HEAD_DIM ** -0.5
    normed_rope(C_DQ, 4, gq_ref, dq_ref, qscale)
    normed_rope(C_DK, 4, gk_ref, dk_ref, 1.0)
    dv = proj(C_DV, 512)
    sv = proj(C_SV, LANES)
    for s in range(tm // ATT_TILE):
        rows = slice(s * ATT_TILE, (s + 1) * ATT_TILE)
        for hd in range(DIFF_HEADS):
            dv_ref[0, hd, s] = dv[rows, hd * LANES:(hd + 1) * LANES].T.astype(dv_ref.dtype)
        sv_ref[0, s] = sv[rows, :].T[:HEAD_DIM].astype(sv_ref.dtype)
    normed_rope(C_SQ, 4, gsq_ref, sq_ref, qscale)
    normed_rope(C_IQ, 4, None, iq_ref, 1.0)
    normed_rope(C_SK, 1, gsk_ref, sk_ref, 1.0)
    normed_rope(C_IK, 1, None, ik_ref, 1.0)
    iw_ref[0] = proj(C_IW, LANES)
    for i in range(4):
        g = proj(C_GATE + i * 512, 512)
        gate_ref[0, :, i * 512:(i + 1) * 512] = jax.nn.sigmoid(g).astype(gate_ref.dtype)


def _inproj(x, sc1, sh1, n1g, pos, consts, w_perm, tm):
    B, T, D = x.shape
    inv128, sgn128, gq, gk, gsq, gsk, gmat = consts
    tok = lambda n, dt: jax.ShapeDtypeStruct((B, T, n), dt)
    blk = lambda n: pl.BlockSpec((1, tm, n), lambda b, i: (b, i, 0))
    cst = lambda a: pl.BlockSpec(a.shape, lambda b, i: (0,) * a.ndim)
    mod = pl.BlockSpec((1, 1, D), lambda b, i: (b, 0, 0))
    n_att, per_step = T // ATT_TILE, tm // ATT_TILE
    dvt_blk = pl.BlockSpec((1, DIFF_HEADS, per_step, LANES, ATT_TILE), lambda b, i: (b, 0, i, 0, 0))
    svt_blk = pl.BlockSpec((1, per_step, HEAD_DIM, ATT_TILE), lambda b, i: (b, i, 0, 0))
    return pl.pallas_call(
        _inproj_kernel,
        grid=(B, T // tm),
        in_specs=[blk(D), mod, mod, cst(n1g), blk(1), cst(inv128), cst(sgn128), cst(gq), cst(gk),
                  cst(gsq), cst(gsk), cst(gmat), cst(w_perm)],
        out_specs=[blk(512), blk(512), dvt_blk, blk(512), blk(512), blk(128), blk(128), svt_blk,
                   blk(128), blk(2048)],
        out_shape=[tok(512, BF16), tok(512, BF16),
                   jax.ShapeDtypeStruct((B, DIFF_HEADS, n_att, LANES, ATT_TILE), BF16),
                   tok(512, BF16), tok(512, BF16), tok(128, BF16), tok(128, BF16),
                   jax.ShapeDtypeStruct((B, n_att, HEAD_DIM, ATT_TILE), BF16),
                   tok(128, F32), tok(2048, BF16)],
        compiler_params=pltpu.CompilerParams(dimension_semantics=("parallel", "parallel"),
                                             vmem_limit_bytes=VMEM_LIMIT),
        name="inproj",
    )(x, sc1, sh1, n1g, pos, inv128, sgn128, gq, gk, gsq, gsk, gmat, w_perm)


def _diff_kernel(q_ref, k_ref, vt_ref, lq1_ref, lk1_ref, lq2_ref, lk2_ref, og_ref, o_ref, *, tq):
    qi = pl.program_id(1)
    lane = lax.broadcasted_iota(I32, (tq, LANES), 1)
    q2 = []
    for h in range(DIFF_HEADS):
        q = q_ref[0, :, h * LANES:(h + 1) * LANES]
        zero = jnp.zeros_like(q)
        q2.append(jnp.concatenate([jnp.where(lane < HEAD_DIM, q, zero),
                                   jnp.where(lane >= HEAD_DIM, q, zero)], axis=0))

    def tile(j, carry, on_diagonal):
        out = []
        for h in range(DIFF_HEADS):
            m, l, acc = carry[h]
            s = _nt(k_ref[0, j, :, h * LANES:(h + 1) * LANES], q2[h])
            if on_diagonal:
                krow = lax.broadcasted_iota(I32, (tq, 2 * tq), 0)
                qcol = lax.broadcasted_iota(I32, (tq, 2 * tq), 1)
                qcol = jnp.where(qcol >= tq, qcol - tq, qcol)
                s = jnp.where(krow // CHUNK <= qcol // CHUNK, s, NEG)
            m_new = jnp.maximum(m, jnp.max(s, axis=0, keepdims=True))
            alpha = jnp.exp(m - m_new)
            p = jnp.exp(s - m_new)
            l = alpha * l + jnp.sum(p, axis=0, keepdims=True)
            acc = alpha * acc + jnp.dot(vt_ref[0, h, j], p.astype(BF16), preferred_element_type=F32)
            out.append((m_new, l, acc))
        return tuple(out)

    init = tuple((jnp.full((1, 2 * tq), NEG, F32), jnp.zeros((1, 2 * tq), F32),
                  jnp.zeros((LANES, 2 * tq), F32)) for _ in range(DIFF_HEADS))
    carry = lax.fori_loop(0, qi, lambda j, c: tile(j, c, False), init)
    res = tile(qi, carry, True)
    lam = (jnp.exp(jnp.sum(lq1_ref[...] * lk1_ref[...], axis=-1, keepdims=True))
           - jnp.exp(jnp.sum(lq2_ref[...] * lk2_ref[...], axis=-1, keepdims=True)) + LAMBDA_INIT)
    for h in range(DIFF_HEADS):
        _, l, acc = res[h]
        o = acc * (1.0 / l)
        o = o[:, :tq] - lam * o[:, tq:]
        ms = jnp.mean(o * o, axis=0, keepdims=True)
        o = o * lax.rsqrt(ms + EPS) * og_ref[...] * (1.0 - LAMBDA_INIT)
        o_ref[0, :, h * LANES:(h + 1) * LANES] = o.T.astype(o_ref.dtype)


def _diff_attention(dq, dk, vt_tiles, lams, out_g):
    B, T, _ = dq.shape
    tq = ATT_TILE
    n = T // tq
    k_tiles = dk.reshape(B, n, tq, DIFF_HEADS * LANES)
    og_cols = jnp.broadcast_to(out_g.reshape(LANES, 1), (LANES, tq))
    width = DIFF_HEADS * LANES
    cst = lambda a: pl.BlockSpec(a.shape, lambda b, i: (0,) * a.ndim)
    return pl.pallas_call(
        functools.partial(_diff_kernel, tq=tq),
        grid=(B, n),
        in_specs=[pl.BlockSpec((1, tq, width), lambda b, i: (b, i, 0)),
                  pl.BlockSpec((1, n, tq, width), lambda b, i: (b, 0, 0, 0)),
                  pl.BlockSpec((1, DIFF_HEADS, n, LANES, tq), lambda b, i: (b, 0, 0, 0, 0)),
                  cst(lams[0]), cst(lams[1]), cst(lams[2]), cst(lams[3]), cst(og_cols)],
        out_specs=pl.BlockSpec((1, tq, width), lambda b, i: (b, i, 0)),
        out_shape=jax.ShapeDtypeStruct((B, T, width), BF16),
        compiler_params=pltpu.CompilerParams(
            dimension_semantics=("parallel", "parallel"), vmem_limit_bytes=VMEM_LIMIT),
        name="diff_attention",
    )(dq, k_tiles, vt_tiles, *lams, og_cols)


def _dsa_kernel(sq_ref, iq_ref, iw_ref, sk_ref, ik_ref, svt_ref, o_ref, key_ref, *, tq, tk, topk, t_bits):
    qi = pl.program_id(1)
    n_kv = (qi * tq + tq + tk - 1) // tk
    lane = lax.broadcasted_iota(I32, (tq, LANES), 1)
    lo_half = lane < HEAD_DIM
    krow = lax.broadcasted_iota(I32, (tk, tq), 0)
    q_chunk = (qi * tq + lax.broadcasted_iota(I32, (tk, tq), 1)) // CHUNK
    idx_scale = (IDX_HEADS ** -0.5) * (HEAD_DIM ** -0.5)
    key_neg_inf = jnp.int32(-2139095041)

    def all_heads(ref):
        out = []
        for p in range(4):
            slab = ref[0, :, p * LANES:(p + 1) * LANES]
            zero = jnp.zeros_like(slab)
            out.append(jnp.where(lo_half, slab, zero))
            out.append(jnp.where(lo_half, zero, slab))
        return jnp.concatenate(out, axis=0)

    def per_head(x, h):
        return x[:, h * tq:(h + 1) * tq]

    iq_all = all_heads(iq_ref)
    iw_t = iw_ref[0].T
    w_all = jnp.concatenate([iw_t[h:h + 1, :] for h in range(IDX_HEADS)], axis=1)

    def score_body(j, carry):
        r = jnp.maximum(_nt(ik_ref[0, j], iq_all), 0.0) * w_all
        score = per_head(r, 0)
        for h in range(1, IDX_HEADS):
            score = score + per_head(r, h)
        score = score * idx_scale + 0.0
        score = jnp.where((j * tk + krow) // CHUNK <= q_chunk, score, -jnp.inf)
        bits = lax.bitcast_convert_type(score, I32)
        key_ref[j] = bits ^ ((bits >> 31) & jnp.int32(0x7FFFFFFF))
        return carry

    lax.fori_loop(0, n_kv, score_body, 0)

    def count(pred_fn):
        def body(j, c):
            hit = jnp.where(pred_fn(key_ref[j], j * tk + krow), 1.0, 0.0)
            parts = [hit[8 * i:8 * i + 8] for i in range(tk // 8)]
            while len(parts) > 1:
                parts = [parts[i] + parts[i + 1] for i in range(0, len(parts), 2)]
            return c + parts[0]
        c = lax.fori_loop(0, n_kv, body, jnp.zeros((8, tq), F32))
        return jnp.sum(c, axis=0, keepdims=True)

    def bit_body(it, t):
        cand = t + jnp.left_shift(jnp.int32(1), 31 - it)
        c = count(lambda kj, _: kj >= cand)
        return jnp.where(c >= float(topk), cand, t)

    thr = lax.fori_loop(0, 32, bit_body, jnp.full((1, tq), jnp.iinfo(jnp.int32).min, I32))
    need = float(topk) - count(lambda kj, _: kj > thr)

    def tie_body(it, jmax):
        cand = jmax + jnp.left_shift(jnp.int32(1), t_bits - 1 - it)
        c = count(lambda kj, kidx: (kj == thr) & (kidx < cand))
        return jnp.where(c < need, cand, jmax)

    ties = count(lambda kj, _: kj == thr)
    some_partial = jnp.max(jnp.where(need < ties, 1.0, 0.0)) > 0.0
    jmax = lax.cond(some_partial,
                    lambda: lax.fori_loop(0, t_bits, tie_body, jnp.zeros((1, tq), I32)),
                    lambda: jnp.full((1, tq), (1 << t_bits) - 1, I32))

    q_all = all_heads(sq_ref)
    width = DSA_HEADS * tq

    def attn_body(j, carry):
        m, l, acc = carry
        kj = key_ref[j]
        sel = ((kj > thr) | ((kj == thr) & (j * tk + krow <= jmax))) & (kj != key_neg_inf)
        sel1 = jnp.where(sel, 1.0, 0.0)
        sel_all = jnp.concatenate([sel1] * DSA_HEADS, axis=1)
        s = jnp.where(sel_all > 0.5, _nt(sk_ref[0, j], q_all), NEG)
        m_new = jnp.maximum(m, jnp.max(s, axis=0, keepdims=True))
        alpha = jnp.exp(m - m_new)
        p = jnp.exp(s - m_new) * sel_all
        l = alpha * l + jnp.sum(p, axis=0, keepdims=True)
        acc = alpha * acc + jnp.dot(svt_ref[0, j], p.astype(BF16), preferred_element_type=F32)
        return m_new, l, acc

    init = (jnp.full((1, width), NEG, F32), jnp.zeros((1, width), F32), jnp.zeros((HEAD_DIM, width), F32))
    _, l, acc = lax.fori_loop(0, n_kv, attn_body, init)
    o_all = acc * (1.0 / l)
    o_t = jnp.concatenate([per_head(o_all, h) for h in range(DSA_HEADS)], axis=0)
    o_ref[0] = o_t.T.astype(o_ref.dtype)


def _dsa_attention(sq, iq, iw, skk, ikk, svt_tiles, tq):
    B, T, _ = sq.shape
    tk = ATT_TILE
    n = T // tk
    topk = min(DSA_TOPK_MAX, T // 4)
    t_bits = max(1, (T - 1).bit_length())
    sk_tiles = skk.reshape(B, n, tk, LANES)
    ik_tiles = ikk.reshape(B, n, tk, LANES)
    qblk = lambda c: pl.BlockSpec((1, tq, c), lambda b, i: (b, i, 0))
    kblk = lambda a: pl.BlockSpec((1,) + a.shape[1:], lambda b, i: (b, 0, 0, 0))
    return pl.pallas_call(
        functools.partial(_dsa_kernel, tq=tq, tk=tk, topk=topk, t_bits=t_bits),
        grid=(B, T // tq),
        in_specs=[qblk(512), qblk(512), qblk(LANES), kblk(sk_tiles), kblk(ik_tiles), kblk(svt_tiles)],
        out_specs=qblk(512),
        out_shape=jax.ShapeDtypeStruct((B, T, DSA_HEADS * HEAD_DIM), BF16),
        scratch_shapes=[pltpu.VMEM((n, tk, tq), I32)],
        compiler_params=pltpu.CompilerParams(dimension_semantics=("parallel", "parallel"),
                                             vmem_limit_bytes=VMEM_LIMIT),
        name="dsa_attention",
    )(sq, iq, iw, sk_tiles, ik_tiles, svt_tiles)


def _top_rows(s, rid, k, payload=None):
    big = float(2 ** 20)
    vals, ids, pay = [], [], []
    for _ in range(k):
        m = jnp.max(s, axis=0, keepdims=True)
        am = jnp.min(jnp.where(s == m, rid, big), axis=0, keepdims=True)
        hit = rid == am
        vals.append(m)
        ids.append(am)
        if payload is not None:
            pay.append(jnp.sum(jnp.where(hit, payload, 0.0), axis=0, keepdims=True))
        s = jnp.where(hit, -jnp.inf, s)
    cat = lambda xs: jnp.concatenate(xs, axis=0)
    return cat(vals), cat(ids), (cat(pay) if payload is not None else None)


def _pair_groups():
    groups = [(0, 0), (0, 8)] + [(a, 0) for a in range(1, 8)]
    return groups, [PEER_TOPK // (a + 1) for a, _ in groups]


def _pair_candidates(v1, i1, v2, i2):
    n = v1.shape[1]
    brow = lax.broadcasted_iota(I32, (8, n), 0)
    groups, limits = _pair_groups()
    cand, cidx, pos = [], [], []
    for (a, b0), lim in zip(groups, limits):
        ok = brow + b0 < lim
        cand.append(jnp.where(ok, v1[a:a + 1] + v2[b0:b0 + 8], -jnp.inf))
        cidx.append(i1[a:a + 1] * float(PEER_NKEYS) + i2[b0:b0 + 8])
        pos.append((brow + (a * PEER_TOPK + b0)).astype(F32))
    cand.append(v1[8:16] + v2[0:1])
    cidx.append(i1[8:16] * float(PEER_NKEYS) + i2[0:1])
    pos.append(((brow + 8) * PEER_TOPK).astype(F32))
    cat = lambda xs: jnp.concatenate(xs, axis=0)
    return cat(cand), cat(cidx), cat(pos)


def _mix_kernel(oa_ref, ob_ref, gate_ref, x_ref, g1_ref, sc_ref, sh_ref, n2_ref, wa_ref, wb_ref, wo_ref,
                wq_ref, sub_ref, x1_ref, h2_ref, eidx_ref, gw_ref, q_ref):
    ya = jnp.dot(oa_ref[0], wa_ref[...], preferred_element_type=F32)
    yb = jnp.dot(ob_ref[0], wb_ref[...], preferred_element_type=F32)
    D = ya.shape[1]
    tm = ya.shape[0]
    merged = gate_ref[0, :, :D].astype(F32) * ya + gate_ref[0, :, D:].astype(F32) * yb
    x1 = x_ref[0] + g1_ref[0] * jnp.dot(merged.astype(BF16), wo_ref[...], preferred_element_type=F32)
    x1_ref[0] = x1
    ms = jnp.mean(x1 * x1, axis=-1, keepdims=True)
    h2 = x1 * lax.rsqrt(ms + EPS) * n2_ref[...] * (1.0 + sc_ref[0]) + sh_ref[0]
    h2_ref[0] = h2
    q_ref[...] = jnp.dot(h2.astype(BF16), wq_ref[...], preferred_element_type=F32).astype(BF16)

    key_id = lax.broadcasted_iota(I32, (PEER_NKEYS, LANES), 0).astype(F32)
    for t in range(tm // LANES):
        cols = slice(t * LANES, (t + 1) * LANES)
        experts, gates_t = [], []
        for h in range(PEER_HEADS):
            tops = []
            for p in range(2):
                hp = 2 * h + p
                s_t = _nt(sub_ref[hp], q_ref[cols, hp * LANES:(hp + 1) * LANES])
                v, i, _ = _top_rows(s_t, key_id, PEER_TOPK)
                tops.append((v, i))
            cand, cidx, pos = _pair_candidates(*tops[0], *tops[1])
            sc, _, e = _top_rows(cand, pos, PEER_TOPK, payload=cidx)
            ex = jnp.exp(sc - sc[0:1])
            g = ex / jnp.sum(ex, axis=0, keepdims=True)
            experts.append(e)
            gates_t.append(g)
        eidx_ref[cols, :] = (jnp.concatenate(experts, axis=0).T * float(SLAB)).astype(I32)
        gw_ref[cols, :] = jnp.concatenate(gates_t, axis=0).T


def _mix(o_a, o_b, gates, x, g1, sc2, sh2, n2g, wa, wb, wo, wq, sub, tm):
    B, T, D = x.shape
    blk = lambda n: pl.BlockSpec((1, tm, n), lambda b, i: (b, i, 0))
    cst = lambda a: pl.BlockSpec(a.shape, lambda b, i: (0,) * a.ndim)
    mod = pl.BlockSpec((1, 1, D), lambda b, i: (b, 0, 0))
    nblk = T // tm
    tblk = pl.BlockSpec((tm, PEER_PAIRS), lambda b, i: (b * nblk + i, 0))
    return pl.pallas_call(
        _mix_kernel,
        grid=(B, nblk),
        in_specs=[blk(512), blk(512), blk(2 * D), blk(D), mod, mod, mod, cst(n2g), cst(wa), cst(wb),
                  cst(wo), cst(wq), cst(sub)],
        out_specs=[blk(D), blk(D), tblk, tblk],
        out_shape=[jax.ShapeDtypeStruct((B, T, D), F32), jax.ShapeDtypeStruct((B, T, D), F32),
                   jax.ShapeDtypeStruct((B * T, PEER_PAIRS), I32),
                   jax.ShapeDtypeStruct((B * T, PEER_PAIRS), F32)],
        scratch_shapes=[pltpu.VMEM((tm, wq.shape[1]), BF16)],
        compiler_params=pltpu.CompilerParams(dimension_semantics=("parallel", "parallel"),
                                             vmem_limit_bytes=VMEM_LIMIT),
        name="mix_route",
    )(o_a, o_b, gates, x, g1, sc2, sh2, n2g, wa, wb, wo, wq, sub)


SLAB = 4
ROW_TILES = 8
GROUP = 8
RING = 4


def _peer_scratch():
    return ([pltpu.SMEM((GROUP, PEER_PAIRS), I32) for _ in range(RING)] + [pltpu.SemaphoreType.DMA((RING,))]
            + [pltpu.VMEM((PEER_PAIRS * SLAB, LANES), I32) for _ in range(2)])


def _pack_table(t):
    n, d = t.shape
    tb = t.astype(BF16).reshape(n, SLAB, 2, LANES).transpose(0, 1, 3, 2)
    return lax.bitcast_convert_type(tb, I32).reshape(n * SLAB, LANES)


def _resident_table_spec(tab):
    return pl.BlockSpec(tab.shape, lambda i: (0, 0), pipeline_mode=pl.Buffered(1))


def _gather_rows(stage_ref, u, tab_ref, w32_ref):
    for k in range(PEER_PAIRS):
        r = pl.multiple_of(stage_ref[u, k], SLAB)
        w32_ref[k * SLAB:(k + 1) * SLAB, :] = tab_ref[pl.ds(r, SLAB), :]


def _staged_token_loop(idx_hbm, stages, sems, tab_ref, compute, bufs, tb):
    step = pl.program_id(0)
    groups_per_step = tb // GROUP
    total = pl.num_programs(0) * groups_per_step
    first = step * groups_per_step

    def copy(g, slot):
        row = pl.multiple_of(jnp.minimum(g, total - 1) * GROUP, GROUP)
        return pltpu.make_async_copy(idx_hbm.at[pl.ds(row, GROUP), :], stages[slot], sems.at[slot])

    @pl.when(step == 0)
    def _():
        for g in range(RING - 1):
            copy(g, g).start()
        copy(0, 0).wait()

    _gather_rows(stages[0], 0, tab_ref, bufs[0])

    def trip(t, carry):
        for r in range(RING):
            g = first + t * RING + r
            for u in range(GROUP):
                nxt = (r, u + 1) if u + 1 < GROUP else ((r + 1) % RING, 0)
                _gather_rows(stages[nxt[0]], nxt[1], tab_ref, bufs[(u + 1) % 2])
                if u == 0:
                    copy(g + 1, (r + 1) % RING).wait()
                    copy(g + RING - 1, (r + RING - 1) % RING).start()
                compute((t * RING + r) * GROUP + u, bufs[u % 2])
        return carry

    lax.fori_loop(0, groups_per_step // RING, trip, 0)

    @pl.when(step == pl.num_programs(0) - 1)
    def _():
        for g in range(1, RING - 1):
            copy(total + g, g % RING).wait()


def _peer_u_kernel(idx_hbm, h2_ref, g_ref, diag_ref, gsum_ref, tab_ref, w_ref, z_ref, *scratch, tb):
    stages, sems, bufs = scratch[:RING], scratch[RING], scratch[RING + 1:]
    diag = diag_ref[...]

    def compute(i, w32_ref):
        wb = pltpu.bitcast(w32_ref[...], BF16)
        x_row = h2_ref[pl.ds(i, 1), :]
        x8 = jnp.concatenate([x_row[:, r * LANES:(r + 1) * LANES] for r in range(ROW_TILES)], axis=0)
        hi, lo = _split_bf16(x8)
        o = _nt(jnp.concatenate([hi, lo], axis=0), wb)
        z_ref[pl.ds(pl.multiple_of(i * ROW_TILES, ROW_TILES), ROW_TILES), :] = (o[:ROW_TILES] + o[ROW_TILES:]) * diag

    _staged_token_loop(idx_hbm, stages, sems, tab_ref, compute, bufs, tb)
    a8 = _dot2(z_ref[...], gsum_ref[...])
    a = jnp.sum(a8.reshape(tb, ROW_TILES, PEER_PAIRS), axis=1)
    gelu = 0.5 * a * (1.0 + lax.erf(a * (2.0 ** -0.5)))
    w_ref[...] = g_ref[...] * gelu


def _peer_v_kernel(idx_hbm, w_ref, x1_ref, g2_ref, diag_ref, gexp_ref, tab_ref, o_ref, wrep_ref, *scratch, tb):
    stages, sems, bufs = scratch[:RING], scratch[RING], scratch[RING + 1:]
    diag = diag_ref[...]
    wrep_ref[...] = _dot2(w_ref[...], gexp_ref[...])
    g2 = g2_ref[0]

    def compute(i, w32_ref):
        wb = pltpu.bitcast(w32_ref[...], BF16)
        a = wrep_ref[pl.ds(i, 1), :] * diag
        hi, lo = _split_bf16(a)
        o = jnp.dot(jnp.concatenate([hi, lo], axis=0), wb, preferred_element_type=F32)
        o8 = o[:ROW_TILES] + o[ROW_TILES:]
        o_row = jnp.concatenate([o8[r:r + 1, :] for r in range(ROW_TILES)], axis=1)
        o_ref[pl.ds(i, 1), :] = x1_ref[pl.ds(i, 1), :] + g2 * o_row

    _staged_token_loop(idx_hbm, stages, sems, tab_ref, compute, bufs, tb)


def _peer_consts():
    c = np.arange(PEER_PAIRS * ROW_TILES)
    diag = (c[None, :] % ROW_TILES == np.arange(ROW_TILES)[:, None]).astype(np.float32)
    gsum = (c[:, None] // ROW_TILES == np.arange(PEER_PAIRS)[None, :]).astype(np.float32)
    return jnp.asarray(diag), jnp.asarray(gsum, BF16), jnp.asarray(gsum.T, BF16)


def _peer_u(idx4, h2, g, utab, tb):
    N, D = h2.shape
    diag, gsum, _ = _peer_consts()
    cst = lambda a: pl.BlockSpec(a.shape, lambda i: (0,) * a.ndim)
    return pl.pallas_call(
        functools.partial(_peer_u_kernel, tb=tb),
        grid=(N // tb,),
        in_specs=[pl.BlockSpec(memory_space=pl.ANY),
                  pl.BlockSpec((tb, D), lambda i: (i, 0)),
                  pl.BlockSpec((tb, PEER_PAIRS), lambda i: (i, 0)),
                  cst(diag), cst(gsum), _resident_table_spec(utab)],
        out_specs=pl.BlockSpec((tb, PEER_PAIRS), lambda i: (i, 0)),
        out_shape=jax.ShapeDtypeStruct((N, PEER_PAIRS), F32),
        scratch_shapes=[pltpu.VMEM((tb * ROW_TILES, PEER_PAIRS * ROW_TILES), F32)] + _peer_scratch(),
        compiler_params=pltpu.CompilerParams(dimension_semantics=("arbitrary",),
                                             vmem_limit_bytes=VMEM_LIMIT),
        name="peer_u",
    )(idx4, h2, g, diag, gsum, utab)


def _peer_v(idx4, w, x1, g2, vtab, tb, blocks_per_batch):
    N, D = x1.shape
    diag, _, gexp = _peer_consts()
    cst = lambda a: pl.BlockSpec(a.shape, lambda i: (0,) * a.ndim)
    return pl.pallas_call(
        functools.partial(_peer_v_kernel, tb=tb),
        grid=(N // tb,),
        in_specs=[pl.BlockSpec(memory_space=pl.ANY),
                  pl.BlockSpec((tb, PEER_PAIRS), lambda i: (i, 0)),
                  pl.BlockSpec((tb, D), lambda i: (i, 0)),
                  pl.BlockSpec((1, 1, D), lambda i: (i // blocks_per_batch, 0, 0)),
                  cst(diag), cst(gexp), _resident_table_spec(vtab)],
        out_specs=pl.BlockSpec((tb, D), lambda i: (i, 0)),
        out_shape=jax.ShapeDtypeStruct((N, D), F32),
        scratch_shapes=[pltpu.VMEM((tb, PEER_PAIRS * ROW_TILES), F32)] + _peer_scratch(),
        compiler_params=pltpu.CompilerParams(dimension_semantics=("arbitrary",),
                                             vmem_limit_bytes=VMEM_LIMIT),
        name="peer_v",
    )(idx4, w, x1, g2, diag, gexp, vtab)


def _permute_w_in(w):
    o = np.cumsum([0, 512, 512, 512, 512, 64, 64, 512, 64, 8, 2048])
    dq, dk, dv, sq, sk, sv, iq, ik, iw, gate = [w[:, o[i]:o[i + 1]] for i in range(10)]
    pad = jnp.zeros((w.shape[0], LANES - IDX_HEADS), w.dtype)
    return jnp.concatenate([dq, dk, dv, sq, iq, sk, sk, ik, ik, sv, sv, iw, pad, gate], axis=1)


def _rope_consts(diff_q_g, diff_k_g, dsa_q_g, dsa_k_g):
    inv = ROPE_THETA ** (-(jnp.arange(0, HEAD_DIM, 2, dtype=F32) / HEAD_DIM))
    inv128 = jnp.tile(inv, LANES // HALF).reshape(1, LANES)
    sgn = np.where(np.arange(LANES) % HEAD_DIM < HALF, -1.0, 1.0).astype(np.float32).reshape(1, LANES)
    grp = np.arange(LANES) // HEAD_DIM
    gmat = jnp.asarray((grp[:, None] == grp[None, :]).astype(np.float32) / HEAD_DIM, BF16)
    t8 = lambda g: jnp.tile(g, 8).reshape(1, 512)
    return (inv128, jnp.asarray(sgn), t8(diff_q_g), t8(diff_k_g), t8(dsa_q_g),
            jnp.tile(dsa_k_g, 2).reshape(1, LANES), gmat)


def kernel(x, c, positions, w_ada, b_ada, norm1_g, w_in, diff_q_g, diff_k_g, diff_lam_q1, diff_lam_k1, diff_lam_q2, diff_lam_k2, diff_out_g, dsa_q_g, dsa_k_g, w_branch_a, w_branch_b, w_out, norm2_g, peer_w_q, peer_sub_keys, peer_u, peer_v):
    B, T, D = x.shape
    N = B * T
    assert w_ada.shape[0] == 1, "single-layer block"
    assert T % ATT_TILE == 0, "sequence length must be a multiple of the attention tile"
    tm_in = min(512, T)
    tm_mix = ATT_TILE
    tq_dsa = 128
    tb = 128

    mod = _ada(c, w_ada[0], b_ada[0])
    sh1, sc1, g1, sh2, sc2, g2 = [m.reshape(B, 1, D) for m in jnp.split(mod, 6, axis=-1)]

    consts = _rope_consts(diff_q_g[0], diff_k_g[0], dsa_q_g[0], dsa_k_g[0])
    w_perm = _permute_w_in(w_in[0]).astype(BF16)
    dq, dk, dvt, sq, iq, skk, ikk, svt, iw, gates = _inproj(
        x, sc1, sh1, norm1_g, positions.reshape(B, T, 1), consts, w_perm, tm_in)

    lams = [v.reshape(1, HEAD_DIM) for v in (diff_lam_q1[0], diff_lam_k1[0], diff_lam_q2[0], diff_lam_k2[0])]
    o_a = _diff_attention(dq, dk, dvt, lams, diff_out_g[0])
    o_b = _dsa_attention(sq, iq, iw, skk, ikk, svt, tq_dsa)

    sub = peer_sub_keys[0].reshape(2 * PEER_HEADS, PEER_NKEYS, LANES).astype(BF16)
    x1, h2, eidx, gw = _mix(o_a, o_b, gates, x, g1, sc2, sh2, norm2_g,
                            w_branch_a[0].astype(BF16), w_branch_b[0].astype(BF16),
                            w_out[0].astype(BF16), peer_w_q[0].astype(BF16), sub, tm_mix)

    w = _peer_u(eidx, h2.reshape(N, D), gw, _pack_table(peer_u[0]), tb)
    out = _peer_v(eidx, w, x1.reshape(N, D), g2, _pack_table(peer_v[0]), tb, T // tb)
    return out.reshape(B, T, D)
```

```python
import functools
import math

import numpy as np
import jax
import jax.numpy as jnp
from jax import lax
from jax.experimental import pallas as pl
from jax.experimental.pallas import tpu as pltpu

F32 = jnp.float32
BF16 = jnp.bfloat16
I32 = jnp.int32

LANES = 128
HEAD_DIM = 64
HALF = HEAD_DIM // 2
CHUNK = 64
ROPE_THETA = 10000.0
EPS = 1e-6
DIFF_HEADS = 4
DSA_HEADS = 8
IDX_HEADS = 8
DSA_TOPK_MAX = 256
PEER_HEADS = 8
PEER_NKEYS = 128
PEER_TOPK = 16
PEER_PAIRS = PEER_HEADS * PEER_TOPK
LAMBDA_INIT = 0.8 - 0.6 * math.exp(-0.3 * 0)
NEG = -1e30
ATT_TILE = 256
VMEM_LIMIT = 56 * 1024 * 1024

C_DQ, C_DK, C_DV, C_SQ, C_IQ = 0, 512, 1024, 1536, 2048
C_SK, C_IK, C_SV, C_IW, C_GATE = 2560, 2688, 2816, 2944, 3072
N_COLS = C_GATE + 2048


def _nt(a, b):
    return lax.dot_general(a, b, (((1,), (1,)), ((), ())), preferred_element_type=F32)


def _split_bf16(x):
    hi = x.astype(BF16)
    lo = (x - hi.astype(F32)).astype(BF16)
    return hi, lo


def _dot2(x, w):
    hi, lo = _split_bf16(x)
    return (jnp.dot(hi, w, preferred_element_type=F32)
            + jnp.dot(lo, w, preferred_element_type=F32))


def _ada_kernel(c_ref, w_ref, b_ref, o_ref):
    c = c_ref[...]
    s = c * jax.nn.sigmoid(c)
    o_ref[...] = jnp.dot(s, w_ref[...], preferred_element_type=F32,
                         precision=lax.Precision.HIGHEST) + b_ref[...]


def _ada(c, w, b):
    B, D = c.shape
    n = w.shape[1]
    bn = 1024
    return pl.pallas_call(
        _ada_kernel,
        grid=(n // bn,),
        in_specs=[pl.BlockSpec((B, D), lambda j: (0, 0)),
                  pl.BlockSpec((D, bn), lambda j: (0, j)),
                  pl.BlockSpec((1, bn), lambda j: (0, j))],
        out_specs=pl.BlockSpec((B, bn), lambda j: (0, j)),
        out_shape=jax.ShapeDtypeStruct((B, n), F32),
        compiler_params=pltpu.CompilerParams(dimension_semantics=("parallel",),
                                             vmem_limit_bytes=VMEM_LIMIT),
        name="ada",
    )(c, w, b.reshape(1, n))


def _rope128(y, cos, sin_signed, first_half):
    partner = jnp.where(first_half, pltpu.roll(y, LANES - HALF, 1), pltpu.roll(y, HALF, 1))
    return y * cos + partner * sin_signed


def _group_ms(y, gmat):
    return _dot2(y * y, gmat)


def _inproj_kernel(x_ref, sc_ref, sh_ref, n1_ref, pos_ref, inv_ref, sgn_ref, gq_ref, gk_ref,
                   gsq_ref, gsk_ref, gmat_ref, w_ref,
                   dq_ref, dk_ref, dv_ref, sq_ref, iq_ref, sk_ref, ik_ref, sv_ref, iw_ref, gate_ref):
    x = x_ref[0]
    ms = jnp.mean(x * x, axis=-1, keepdims=True)
    h = x * lax.rsqrt(ms + EPS) * n1_ref[...] * (1.0 + sc_ref[0]) + sh_ref[0]
    hb = h.astype(BF16)
    tm = x.shape[0]

    ang = pos_ref[0].astype(F32) * inv_ref[...]
    cos = jnp.cos(ang)
    sin = jnp.sin(ang) * sgn_ref[...]
    lane = lax.broadcasted_iota(I32, (tm, LANES), 1)
    first_half = (lane % HEAD_DIM) < HALF
    gmat = gmat_ref[...]

    def proj(c0, n):
        return jnp.dot(hb, w_ref[:, c0:c0 + n], preferred_element_type=F32)

    def normed_rope(c0, nblk, gain_ref, out_ref, scale):
        for i in range(nblk):
            y = proj(c0 + i * LANES, LANES)
            if gain_ref is not None:
                y = y * lax.rsqrt(_group_ms(y, gmat) + EPS) * gain_ref[:, i * LANES:(i + 1) * LANES]
            y = _rope128(y, cos, sin, first_half)
            if scale != 1.0:
                y = y * scale
            out_ref[0, :, i * LANES:(i + 1) * LANES] = y.astype(out_ref.dtype)

    qscale = HEAD_DIM ** -0.5
    normed_rope(C_DQ, 4, gq_ref, dq_ref, qscale)
    normed_rope(C_DK, 4, gk_ref, dk_ref, 1.0)
    dv = proj(C_DV, 512)
    sv = proj(C_SV, LANES)
    for s in range(tm // ATT_TILE):
        rows = slice(s * ATT_TILE, (s + 1) * ATT_TILE)
        for hd in range(DIFF_HEADS):
            dv_ref[0, hd, s] = dv[rows, hd * LANES:(hd + 1) * LANES].T.astype(dv_ref.dtype)
        sv_ref[0, s] = sv[rows, :].T[:HEAD_DIM].astype(sv_ref.dtype)
    normed_rope(C_SQ, 4, gsq_ref, sq_ref, qscale)
    normed_rope(C_IQ, 4, None, iq_ref, 1.0)
    normed_rope(C_SK, 1, gsk_ref, sk_ref, 1.0)
    normed_rope(C_IK, 1, None, ik_ref, 1.0)
    iw_ref[0] = proj(C_IW, LANES)
    for i in range(4):
        g = proj(C_GATE + i * 512, 512)
        gate_ref[0, :, i * 512:(i + 1) * 512] = jax.nn.sigmoid(g).astype(gate_ref.dtype)


def _inproj(x, sc1, sh1, n1g, pos, consts, w_perm, tm):
    B, T, D = x.shape
    inv128, sgn128, gq, gk, gsq, gsk, gmat = consts
    tok = lambda n, dt: jax.ShapeDtypeStruct((B, T, n), dt)
    blk = lambda n: pl.BlockSpec((1, tm, n), lambda b, i: (b, i, 0))
    cst = lambda a: pl.BlockSpec(a.shape, lambda b, i: (0,) * a.ndim)
    mod = pl.BlockSpec((1, 1, D), lambda b, i: (b, 0, 0))
    n_att, per_step = T // ATT_TILE, tm // ATT_TILE
    dvt_blk = pl.BlockSpec((1, DIFF_HEADS, per_step, LANES, ATT_TILE), lambda b, i: (b, 0, i, 0, 0))
    svt_blk = pl.BlockSpec((1, per_step, HEAD_DIM, ATT_TILE), lambda b, i: (b, i, 0, 0))
    return pl.pallas_call(
        _inproj_kernel,
        grid=(B, T // tm),
        in_specs=[blk(D), mod, mod, cst(n1g), blk(1), cst(inv128), cst(sgn128), cst(gq), cst(gk),
                  cst(gsq), cst(gsk), cst(gmat), cst(w_perm)],
        out_specs=[blk(512), blk(512), dvt_blk, blk(512), blk(512), blk(128), blk(128), svt_blk,
                   blk(128), blk(2048)],
        out_shape=[tok(512, BF16), tok(512, BF16),
                   jax.ShapeDtypeStruct((B, DIFF_HEADS, n_att, LANES, ATT_TILE), BF16),
                   tok(512, BF16), tok(512, BF16), tok(128, BF16), tok(128, BF16),
                   jax.ShapeDtypeStruct((B, n_att, HEAD_DIM, ATT_TILE), BF16),
                   tok(128, F32), tok(2048, BF16)],
        compiler_params=pltpu.CompilerParams(dimension_semantics=("parallel", "parallel"),
                                             vmem_limit_bytes=VMEM_LIMIT),
        name="inproj",
    )(x, sc1, sh1, n1g, pos, inv128, sgn128, gq, gk, gsq, gsk, gmat, w_perm)


def _diff_kernel(q_ref, k_ref, vt_ref, lq1_ref, lk1_ref, lq2_ref, lk2_ref, og_ref, o_ref, *, tq):
    qi = pl.program_id(1)
    lane = lax.broadcasted_iota(I32, (tq, LANES), 1)
    q2 = []
    for h in range(DIFF_HEADS):
        q = q_ref[0, :, h * LANES:(h + 1) * LANES]
        zero = jnp.zeros_like(q)
        q2.append(jnp.concatenate([jnp.where(lane < HEAD_DIM, q, zero),
                                   jnp.where(lane >= HEAD_DIM, q, zero)], axis=0))

    def tile(j, carry, on_diagonal):
        out = []
        for h in range(DIFF_HEADS):
            m, l, acc = carry[h]
            s = _nt(k_ref[0, j, :, h * LANES:(h + 1) * LANES], q2[h])
            if on_diagonal:
                krow = lax.broadcasted_iota(I32, (tq, 2 * tq), 0)
                qcol = lax.broadcasted_iota(I32, (tq, 2 * tq), 1)
                qcol = jnp.where(qcol >= tq, qcol - tq, qcol)
                s = jnp.where(krow // CHUNK <= qcol // CHUNK, s, NEG)
            m_new = jnp.maximum(m, jnp.max(s, axis=0, keepdims=True))
            alpha = jnp.exp(m - m_new)
            p = jnp.exp(s - m_new)
            l = alpha * l + jnp.sum(p, axis=0, keepdims=True)
            acc = alpha * acc + jnp.dot(vt_ref[0, h, j], p.astype(BF16), preferred_element_type=F32)
            out.append((m_new, l, acc))
        return tuple(out)

    init = tuple((jnp.full((1, 2 * tq), NEG, F32), jnp.zeros((1, 2 * tq), F32),
                  jnp.zeros((LANES, 2 * tq), F32)) for _ in range(DIFF_HEADS))
    carry = lax.fori_loop(0, qi, lambda j, c: tile(j, c, False), init)
    res = tile(qi, carry, True)
    lam = (jnp.exp(jnp.sum(lq1_ref[...] * lk1_ref[...], axis=-1, keepdims=True))
           - jnp.exp(jnp.sum(lq2_ref[...] * lk2_ref[...], axis=-1, keepdims=True)) + LAMBDA_INIT)
    for h in range(DIFF_HEADS):
        _, l, acc = res[h]
        o = acc * (1.0 / l)
        o = o[:, :tq] - lam * o[:, tq:]
        ms = jnp.mean(o * o, axis=0, keepdims=True)
        o = o * lax.rsqrt(ms + EPS) * og_ref[...] * (1.0 - LAMBDA_INIT)
        o_ref[0, :, h * LANES:(h + 1) * LANES] = o.T.astype(o_ref.dtype)


def _diff_attention(dq, dk, vt_tiles, lams, out_g):
    B, T, _ = dq.shape
    tq = ATT_TILE
    n = T // tq
    k_tiles = dk.reshape(B, n, tq, DIFF_HEADS * LANES)
    og_cols = jnp.broadcast_to(out_g.reshape(LANES, 1), (LANES, tq))
    width = DIFF_HEADS * LANES
    cst = lambda a: pl.BlockSpec(a.shape, lambda b, i: (0,) * a.ndim)
    return pl.pallas_call(
        functools.partial(_diff_kernel, tq=tq),
        grid=(B, n),
        in_specs=[pl.BlockSpec((1, tq, width), lambda b, i: (b, i, 0)),
                  pl.BlockSpec((1, n, tq, width), lambda b, i: (b, 0, 0, 0)),
                  pl.BlockSpec((1, DIFF_HEADS, n, LANES, tq), lambda b, i: (b, 0, 0, 0, 0)),
                  cst(lams[0]), cst(lams[1]), cst(lams[2]), cst(lams[3]), cst(og_cols)],
        out_specs=pl.BlockSpec((1, tq, width), lambda b, i: (b, i, 0)),
        out_shape=jax.ShapeDtypeStruct((B, T, width), BF16),
        compiler_params=pltpu.CompilerParams(
            dimension_semantics=("parallel", "parallel"), vmem_limit_bytes=VMEM_LIMIT),
        name="diff_attention",
    )(dq, k_tiles, vt_tiles, *lams, og_cols)


def _dsa_kernel(sq_ref, iq_ref, iw_ref, sk_ref, ik_ref, svt_ref, o_ref, key_ref, *, tq, tk, topk, t_bits):
    qi = pl.program_id(1)
    n_kv = (qi * tq + tq + tk - 1) // tk
    lane = lax.broadcasted_iota(I32, (tq, LANES), 1)
    lo_half = lane < HEAD_DIM
    krow = lax.broadcasted_iota(I32, (tk, tq), 0)
    q_chunk = (qi * tq + lax.broadcasted_iota(I32, (tk, tq), 1)) // CHUNK
    idx_scale = (IDX_HEADS ** -0.5) * (HEAD_DIM ** -0.5)
    key_neg_inf = jnp.int32(-2139095041)

    def all_heads(ref):
        out = []
        for p in range(4):
            slab = ref[0, :, p * LANES:(p + 1) * LANES]
            zero = jnp.zeros_like(slab)
            out.append(jnp.where(lo_half, slab, zero))
            out.append(jnp.where(lo_half, zero, slab))
        return jnp.concatenate(out, axis=0)

    def per_head(x, h):
        return x[:, h * tq:(h + 1) * tq]

    iq_all = all_heads(iq_ref)
    iw_t = iw_ref[0].T
    w_all = jnp.concatenate([iw_t[h:h + 1, :] for h in range(IDX_HEADS)], axis=1)

    def score_body(j, carry):
        r = jnp.maximum(_nt(ik_ref[0, j], iq_all), 0.0) * w_all
        score = per_head(r, 0)
        for h in range(1, IDX_HEADS):
            score = score + per_head(r, h)
        score = score * idx_scale + 0.0
        score = jnp.where((j * tk + krow) // CHUNK <= q_chunk, score, -jnp.inf)
        bits = lax.bitcast_convert_type(score, I32)
        key_ref[j] = bits ^ ((bits >> 31) & jnp.int32(0x7FFFFFFF))
        return carry

    lax.fori_loop(0, n_kv, score_body, 0)

    def count(pred_fn):
        def body(j, c):
            hit = jnp.where(pred_fn(key_ref[j], j * tk + krow), 1.0, 0.0)
            parts = [hit[8 * i:8 * i + 8] for i in range(tk // 8)]
            while len(parts) > 1:
                parts = [parts[i] + parts[i + 1] for i in range(0, len(parts), 2)]
            return c + parts[0]
        c = lax.fori_loop(0, n_kv, body, jnp.zeros((8, tq), F32))
        return jnp.sum(c, axis=0, keepdims=True)

    def bit_body(it, t):
        cand = t + jnp.left_shift(jnp.int32(1), 31 - it)
        c = count(lambda kj, _: kj >= cand)
        return jnp.where(c >= float(topk), cand, t)

    thr = lax.fori_loop(0, 32, bit_body, jnp.full((1, tq), jnp.iinfo(jnp.int32).min, I32))
    need = float(topk) - count(lambda kj, _: kj > thr)

    def tie_body(it, jmax):
        cand = jmax + jnp.left_shift(jnp.int32(1), t_bits - 1 - it)
        c = count(lambda kj, kidx: (kj == thr) & (kidx < cand))
        return jnp.where(c < need, cand, jmax)

    ties = count(lambda kj, _: kj == thr)
    some_partial = jnp.max(jnp.where(need < ties, 1.0, 0.0)) > 0.0
    jmax = lax.cond(some_partial,
                    lambda: lax.fori_loop(0, t_bits, tie_body, jnp.zeros((1, tq), I32)),
                    lambda: jnp.full((1, tq), (1 << t_bits) - 1, I32))

    q_all = all_heads(sq_ref)
    width = DSA_HEADS * tq

    def attn_body(j, carry):
        m, l, acc = carry
        kj = key_ref[j]
        sel = ((kj > thr) | ((kj == thr) & (j * tk + krow <= jmax))) & (kj != key_neg_inf)
        sel1 = jnp.where(sel, 1.0, 0.0)
        sel_all = jnp.concatenate([sel1] * DSA_HEADS, axis=1)
        s = jnp.where(sel_all > 0.5, _nt(sk_ref[0, j], q_all), NEG)
        m_new = jnp.maximum(m, jnp.max(s, axis=0, keepdims=True))
        alpha = jnp.exp(m - m_new)
        p = jnp.exp(s - m_new) * sel_all
        l = alpha * l + jnp.sum(p, axis=0, keepdims=True)
        acc = alpha * acc + jnp.dot(svt_ref[0, j], p.astype(BF16), preferred_element_type=F32)
        return m_new, l, acc

    init = (jnp.full((1, width), NEG, F32), jnp.zeros((1, width), F32), jnp.zeros((HEAD_DIM, width), F32))
    _, l, acc = lax.fori_loop(0, n_kv, attn_body, init)
    o_all = acc * (1.0 / l)
    o_t = jnp.concatenate([per_head(o_all, h) for h in range(DSA_HEADS)], axis=0)
    o_ref[0] = o_t.T.astype(o_ref.dtype)


def _dsa_attention(sq, iq, iw, skk, ikk, svt_tiles, tq):
    B, T, _ = sq.shape
    tk = ATT_TILE
    n = T // tk
    topk = min(DSA_TOPK_MAX, T // 4)
    t_bits = max(1, (T - 1).bit_length())
    sk_tiles = skk.reshape(B, n, tk, LANES)
    ik_tiles = ikk.reshape(B, n, tk, LANES)
    qblk = lambda c: pl.BlockSpec((1, tq, c), lambda b, i: (b, i, 0))
    kblk = lambda a: pl.BlockSpec((1,) + a.shape[1:], lambda b, i: (b, 0, 0, 0))
    return pl.pallas_call(
        functools.partial(_dsa_kernel, tq=tq, tk=tk, topk=topk, t_bits=t_bits),
        grid=(B, T // tq),
        in_specs=[qblk(512), qblk(512), qblk(LANES), kblk(sk_tiles), kblk(ik_tiles), kblk(svt_tiles)],
        out_specs=qblk(512),
        out_shape=jax.ShapeDtypeStruct((B, T, DSA_HEADS * HEAD_DIM), BF16),
        scratch_shapes=[pltpu.VMEM((n, tk, tq), I32)],
        compiler_params=pltpu.CompilerParams(dimension_semantics=("parallel", "parallel"),
                                             vmem_limit_bytes=VMEM_LIMIT),
        name="dsa_attention",
    )(sq, iq, iw, sk_tiles, ik_tiles, svt_tiles)


def _top_rows(s, rid, k, payload=None):
    big = float(2 ** 20)
    vals, ids, pay = [], [], []
    for _ in range(k):
        m = jnp.max(s, axis=0, keepdims=True)
        am = jnp.min(jnp.where(s == m, rid, big), axis=0, keepdims=True)
        hit = rid == am
        vals.append(m)
        ids.append(am)
        if payload is not None:
            pay.append(jnp.sum(jnp.where(hit, payload, 0.0), axis=0, keepdims=True))
        s = jnp.where(hit, -jnp.inf, s)
    cat = lambda xs: jnp.concatenate(xs, axis=0)
    return cat(vals), cat(ids), (cat(pay) if payload is not None else None)


def _pair_groups():
    groups = [(0, 0), (0, 8)] + [(a, 0) for a in range(1, 8)]
    return groups, [PEER_TOPK // (a + 1) for a, _ in groups]


def _pair_candidates(v1, i1, v2, i2):
    n = v1.shape[1]
    brow = lax.broadcasted_iota(I32, (8, n), 0)
    groups, limits = _pair_groups()
    cand, cidx, pos = [], [], []
    for (a, b0), lim in zip(groups, limits):
        ok = brow + b0 < lim
        cand.append(jnp.where(ok, v1[a:a + 1] + v2[b0:b0 + 8], -jnp.inf))
        cidx.append(i1[a:a + 1] * float(PEER_NKEYS) + i2[b0:b0 + 8])
        pos.append((brow + (a * PEER_TOPK + b0)).astype(F32))
    cand.append(v1[8:16] + v2[0:1])
    cidx.append(i1[8:16] * float(PEER_NKEYS) + i2[0:1])
    pos.append(((brow + 8) * PEER_TOPK).astype(F32))
    cat = lambda xs: jnp.concatenate(xs, axis=0)
    return cat(cand), cat(cidx), cat(pos)


def _mix_kernel(oa_ref, ob_ref, gate_ref, x_ref, g1_ref, sc_ref, sh_ref, n2_ref, wa_ref, wb_ref, wo_ref,
                wq_ref, sub_ref, x1_ref, h2_ref, eidx_ref, gw_ref, q_ref):
    ya = jnp.dot(oa_ref[0], wa_ref[...], preferred_element_type=F32)
    yb = jnp.dot(ob_ref[0], wb_ref[...], preferred_element_type=F32)
    D = ya.shape[1]
    tm = ya.shape[0]
    merged = gate_ref[0, :, :D].astype(F32) * ya + gate_ref[0, :, D:].astype(F32) * yb
    x1 = x_ref[0] + g1_ref[0] * jnp.dot(merged.astype(BF16), wo_ref[...], preferred_element_type=F32)
    x1_ref[0] = x1
    ms = jnp.mean(x1 * x1, axis=-1, keepdims=True)
    h2 = x1 * lax.rsqrt(ms + EPS) * n2_ref[...] * (1.0 + sc_ref[0]) + sh_ref[0]
    h2_ref[0] = h2
    q_ref[...] = jnp.dot(h2.astype(BF16), wq_ref[...], preferred_element_type=F32).astype(BF16)

    key_id = lax.broadcasted_iota(I32, (PEER_NKEYS, LANES), 0).astype(F32)
    for t in range(tm // LANES):
        cols = slice(t * LANES, (t + 1) * LANES)
        experts, gates_t = [], []
        for h in range(PEER_HEADS):
            tops = []
            for p in range(2):
                hp = 2 * h + p
                s_t = _nt(sub_ref[hp], q_ref[cols, hp * LANES:(hp + 1) * LANES])
                v, i, _ = _top_rows(s_t, key_id, PEER_TOPK)
                tops.append((v, i))
            cand, cidx, pos = _pair_candidates(*tops[0], *tops[1])
            sc, _, e = _top_rows(cand, pos, PEER_TOPK, payload=cidx)
            ex = jnp.exp(sc - sc[0:1])
            g = ex / jnp.sum(ex, axis=0, keepdims=True)
            experts.append(e)
            gates_t.append(g)
        eidx_ref[cols, :] = (jnp.concatenate(experts, axis=0).T * float(SLAB)).astype(I32)
        gw_ref[cols, :] = jnp.concatenate(gates_t, axis=0).T


def _mix(o_a, o_b, gates, x, g1, sc2, sh2, n2g, wa, wb, wo, wq, sub, tm):
    B, T, D = x.shape
    blk = lambda n: pl.BlockSpec((1, tm, n), lambda b, i: (b, i, 0))
    cst = lambda a: pl.BlockSpec(a.shape, lambda b, i: (0,) * a.ndim)
    mod = pl.BlockSpec((1, 1, D), lambda b, i: (b, 0, 0))
    nblk = T // tm
    tblk = pl.BlockSpec((tm, PEER_PAIRS), lambda b, i: (b * nblk + i, 0))
    return pl.pallas_call(
        _mix_kernel,
        grid=(B, nblk),
        in_specs=[blk(512), blk(512), blk(2 * D), blk(D), mod, mod, mod, cst(n2g), cst(wa), cst(wb),
                  cst(wo), cst(wq), cst(sub)],
        out_specs=[blk(D), blk(D), tblk, tblk],
        out_shape=[jax.ShapeDtypeStruct((B, T, D), F32), jax.ShapeDtypeStruct((B, T, D), F32),
                   jax.ShapeDtypeStruct((B * T, PEER_PAIRS), I32),
                   jax.ShapeDtypeStruct((B * T, PEER_PAIRS), F32)],
        scratch_shapes=[pltpu.VMEM((tm, wq.shape[1]), BF16)],
        compiler_params=pltpu.CompilerParams(dimension_semantics=("parallel", "parallel"),
                                             vmem_limit_bytes=VMEM_LIMIT),
        name="mix_route",
    )(o_a, o_b, gates, x, g1, sc2, sh2, n2g, wa, wb, wo, wq, sub)


SLAB = 4
ROW_TILES = 8
GROUP = 8
RING = 4


def _peer_scratch():
    return ([pltpu.SMEM((GROUP, PEER_PAIRS), I32) for _ in range(RING)] + [pltpu.SemaphoreType.DMA((RING,))]
            + [pltpu.VMEM((PEER_PAIRS * SLAB, LANES), I32) for _ in range(2)])


def _pack_table(t):
    n, d = t.shape
    tb = t.astype(BF16).reshape(n, SLAB, 2, LANES).transpose(0, 1, 3, 2)
    return lax.bitcast_convert_type(tb, I32).reshape(n * SLAB, LANES)


def _resident_table_spec(tab):
    return pl.BlockSpec(tab.shape, lambda i: (0, 0), pipeline_mode=pl.Buffered(1))


def _gather_rows(stage_ref, u, tab_ref, w32_ref):
    for k in range(PEER_PAIRS):
        r = pl.multiple_of(stage_ref[u, k], SLAB)
        w32_ref[k * SLAB:(k + 1) * SLAB, :] = tab_ref[pl.ds(r, SLAB), :]


def _staged_token_loop(idx_hbm, stages, sems, tab_ref, compute, bufs, tb):
    step = pl.program_id(0)
    groups_per_step = tb // GROUP
    total = pl.num_programs(0) * groups_per_step
    first = step * groups_per_step

    def copy(g, slot):
        row = pl.multiple_of(jnp.minimum(g, total - 1) * GROUP, GROUP)
        return pltpu.make_async_copy(idx_hbm.at[pl.ds(row, GROUP), :], stages[slot], sems.at[slot])

    @pl.when(step == 0)
    def _():
        for g in range(RING - 1):
            copy(g, g).start()
        copy(0, 0).wait()

    _gather_rows(stages[0], 0, tab_ref, bufs[0])

    def trip(t, carry):
        for r in range(RING):
            g = first + t * RING + r
            for u in range(GROUP):
                nxt = (r, u + 1) if u + 1 < GROUP else ((r + 1) % RING, 0)
                _gather_rows(stages[nxt[0]], nxt[1], tab_ref, bufs[(u + 1) % 2])
                if u == 0:
                    copy(g + 1, (r + 1) % RING).wait()
                    copy(g + RING - 1, (r + RING - 1) % RING).start()
                compute((t * RING + r) * GROUP + u, bufs[u % 2])
        return carry

    lax.fori_loop(0, groups_per_step // RING, trip, 0)

    @pl.when(step == pl.num_programs(0) - 1)
    def _():
        for g in range(1, RING - 1):
            copy(total + g, g % RING).wait()


def _peer_u_kernel(idx_hbm, h2_ref, g_ref, diag_ref, gsum_ref, tab_ref, w_ref, z_ref, *scratch, tb):
    stages, sems, bufs = scratch[:RING], scratch[RING], scratch[RING + 1:]
    diag = diag_ref[...]

    def compute(i, w32_ref):
        wb = pltpu.bitcast(w32_ref[...], BF16)
        x_row = h2_ref[pl.ds(i, 1), :]
        x8 = jnp.concatenate([x_row[:, r * LANES:(r + 1) * LANES] for r in range(ROW_TILES)], axis=0)
        hi, lo = _split_bf16(x8)
        o = _nt(jnp.concatenate([hi, lo], axis=0), wb)
        z_ref[pl.ds(i, 1), :] = jnp.sum((o[:ROW_TILES] + o[ROW_TILES:]) * diag, axis=0, keepdims=True)

    _staged_token_loop(idx_hbm, stages, sems, tab_ref, compute, bufs, tb)
    a = _dot2(z_ref[...], gsum_ref[...])
    gelu = 0.5 * a * (1.0 + lax.erf(a * (2.0 ** -0.5)))
    w_ref[...] = g_ref[...] * gelu


def _peer_v_kernel(idx_hbm, w_ref, x1_ref, g2_ref, diag_ref, gexp_ref, tab_ref, o_ref, wrep_ref, *scratch, tb):
    stages, sems, bufs = scratch[:RING], scratch[RING], scratch[RING + 1:]
    diag = diag_ref[...]
    wrep_ref[...] = _dot2(w_ref[...], gexp_ref[...])
    g2 = g2_ref[0]

    def compute(i, w32_ref):
        wb = pltpu.bitcast(w32_ref[...], BF16)
        a = wrep_ref[pl.ds(i, 1), :] * diag
        hi, lo = _split_bf16(a)
        o = jnp.dot(jnp.concatenate([hi, lo], axis=0), wb, preferred_element_type=F32)
        o8 = o[:ROW_TILES] + o[ROW_TILES:]
        o_row = jnp.concatenate([o8[r:r + 1, :] for r in range(ROW_TILES)], axis=1)
        o_ref[pl.ds(i, 1), :] = x1_ref[pl.ds(i, 1), :] + g2 * o_row

    _staged_token_loop(idx_hbm, stages, sems, tab_ref, compute, bufs, tb)


def _peer_consts():
    c = np.arange(PEER_PAIRS * ROW_TILES)
    diag = (c[None, :] % ROW_TILES == np.arange(ROW_TILES)[:, None]).astype(np.float32)
    gsum = (c[:, None] // ROW_TILES == np.arange(PEER_PAIRS)[None, :]).astype(np.float32)
    return jnp.asarray(diag), jnp.asarray(gsum, BF16), jnp.asarray(gsum.T, BF16)


def _peer_u(idx4, h2, g, utab, tb):
    N, D = h2.shape
    diag, gsum, _ = _peer_consts()
    cst = lambda a: pl.BlockSpec(a.shape, lambda i: (0,) * a.ndim)
    return pl.pallas_call(
        functools.partial(_peer_u_kernel, tb=tb),
        grid=(N // tb,),
        in_specs=[pl.BlockSpec(memory_space=pl.ANY),
                  pl.BlockSpec((tb, D), lambda i: (i, 0)),
                  pl.BlockSpec((tb, PEER_PAIRS), lambda i: (i, 0)),
                  cst(diag), cst(gsum), _resident_table_spec(utab)],
        out_specs=pl.BlockSpec((tb, PEER_PAIRS), lambda i: (i, 0)),
        out_shape=jax.ShapeDtypeStruct((N, PEER_PAIRS), F32),
        scratch_shapes=[pltpu.VMEM((tb, PEER_PAIRS * ROW_TILES), F32)] + _peer_scratch(),
        compiler_params=pltpu.CompilerParams(dimension_semantics=("arbitrary",),
                                             vmem_limit_bytes=VMEM_LIMIT),
        name="peer_u",
    )(idx4, h2, g, diag, gsum, utab)


def _peer_v(idx4, w, x1, g2, vtab, tb, blocks_per_batch):
    N, D = x1.shape
    diag, _, gexp = _peer_consts()
    cst = lambda a: pl.BlockSpec(a.shape, lambda i: (0,) * a.ndim)
    return pl.pallas_call(
        functools.partial(_peer_v_kernel, tb=tb),
        grid=(N // tb,),
        in_specs=[pl.BlockSpec(memory_space=pl.ANY),
                  pl.BlockSpec((tb, PEER_PAIRS), lambda i: (i, 0)),
                  pl.BlockSpec((tb, D), lambda i: (i, 0)),
                  pl.BlockSpec((1, 1, D), lambda i: (i // blocks_per_batch, 0, 0)),
                  cst(diag), cst(gexp), _resident_table_spec(vtab)],
        out_specs=pl.BlockSpec((tb, D), lambda i: (i, 0)),
        out_shape=jax.ShapeDtypeStruct((N, D), F32),
        scratch_shapes=[pltpu.VMEM((tb, PEER_PAIRS * ROW_TILES), F32)] + _peer_scratch(),
        compiler_params=pltpu.CompilerParams(dimension_semantics=("arbitrary",),
                                             vmem_limit_bytes=VMEM_LIMIT),
        name="peer_v",
    )(idx4, w, x1, g2, diag, gexp, vtab)


def _permute_w_in(w):
    o = np.cumsum([0, 512, 512, 512, 512, 64, 64, 512, 64, 8, 2048])
    dq, dk, dv, sq, sk, sv, iq, ik, iw, gate = [w[:, o[i]:o[i + 1]] for i in range(10)]
    pad = jnp.zeros((w.shape[0], LANES - IDX_HEADS), w.dtype)
    return jnp.concatenate([dq, dk, dv, sq, iq, sk, sk, ik, ik, sv, sv, iw, pad, gate], axis=1)


def _rope_consts(diff_q_g, diff_k_g, dsa_q_g, dsa_k_g):
    inv = ROPE_THETA ** (-(jnp.arange(0, HEAD_DIM, 2, dtype=F32) / HEAD_DIM))
    inv128 = jnp.tile(inv, LANES // HALF).reshape(1, LANES)
    sgn = np.where(np.arange(LANES) % HEAD_DIM < HALF, -1.0, 1.0).astype(np.float32).reshape(1, LANES)
    grp = np.arange(LANES) // HEAD_DIM
    gmat = jnp.asarray((grp[:, None] == grp[None, :]).astype(np.float32) / HEAD_DIM, BF16)
    t8 = lambda g: jnp.tile(g, 8).reshape(1, 512)
    return (inv128, jnp.asarray(sgn), t8(diff_q_g), t8(diff_k_g), t8(dsa_q_g),
            jnp.tile(dsa_k_g, 2).reshape(1, LANES), gmat)


def kernel(x, c, positions, w_ada, b_ada, norm1_g, w_in, diff_q_g, diff_k_g, diff_lam_q1, diff_lam_k1, diff_lam_q2, diff_lam_k2, diff_out_g, dsa_q_g, dsa_k_g, w_branch_a, w_branch_b, w_out, norm2_g, peer_w_q, peer_sub_keys, peer_u, peer_v):
    B, T, D = x.shape
    N = B * T
    assert w_ada.shape[0] == 1, "single-layer block"
    assert T % ATT_TILE == 0, "sequence length must be a multiple of the attention tile"
    tm_in = min(512, T)
    tm_mix = ATT_TILE
    tq_dsa = 128
    tb = 256

    mod = _ada(c, w_ada[0], b_ada[0])
    sh1, sc1, g1, sh2, sc2, g2 = [m.reshape(B, 1, D) for m in jnp.split(mod, 6, axis=-1)]

    consts = _rope_consts(diff_q_g[0], diff_k_g[0], dsa_q_g[0], dsa_k_g[0])
    w_perm = _permute_w_in(w_in[0]).astype(BF16)
    dq, dk, dvt, sq, iq, skk, ikk, svt, iw, gates = _inproj(
        x, sc1, sh1, norm1_g, positions.reshape(B, T, 1), consts, w_perm, tm_in)

    lams = [v.reshape(1, HEAD_DIM) for v in (diff_lam_q1[0], diff_lam_k1[0], diff_lam_q2[0], diff_lam_k2[0])]
    o_a = _diff_attention(dq, dk, dvt, lams, diff_out_g[0])
    o_b = _dsa_attention(sq, iq, iw, skk, ikk, svt, tq_dsa)

    sub = peer_sub_keys[0].reshape(2 * PEER_HEADS, PEER_NKEYS, LANES).astype(BF16)
    x1, h2, eidx, gw = _mix(o_a, o_b, gates, x, g1, sc2, sh2, norm2_g,
                            w_branch_a[0].astype(BF16), w_branch_b[0].astype(BF16),
                            w_out[0].astype(BF16), peer_w_q[0].astype(BF16), sub, tm_mix)

    w = _peer_u(eidx, h2.reshape(N, D), gw, _pack_table(peer_u[0]), tb)
    out = _peer_v(eidx, w, x1.reshape(N, D), g2, _pack_table(peer_v[0]), tb, T // tb)
    return out.reshape(B, T, D)
```

```python
import functools
import math

import numpy as np
import jax
import jax.numpy as jnp
from jax import lax
from jax.experimental import pallas as pl
from jax.experimental.pallas import tpu as pltpu

F32 = jnp.float32
BF16 = jnp.bfloat16
I32 = jnp.int32

LANES = 128
HEAD_DIM = 64
HALF = HEAD_DIM // 2
CHUNK = 64
ROPE_THETA = 10000.0
EPS = 1e-6
DIFF_HEADS = 4
DSA_HEADS = 8
IDX_HEADS = 8
DSA_TOPK_MAX = 256
PEER_HEADS = 8
PEER_NKEYS = 128
PEER_TOPK = 16
PEER_PAIRS = PEER_HEADS * PEER_TOPK
LAMBDA_INIT = 0.8 - 0.6 * math.exp(-0.3 * 0)
NEG = -1e30
ATT_TILE = 256
VMEM_LIMIT = 56 * 1024 * 1024

C_DQ, C_DK, C_DV, C_SQ, C_IQ = 0, 512, 1024, 1536, 2048
C_SK, C_IK, C_SV, C_IW, C_GATE = 2560, 2688, 2816, 2944, 3072
N_COLS = C_GATE + 2048


def _nt(a, b):
    return lax.dot_general(a, b, (((1,), (1,)), ((), ())), preferred_element_type=F32)


def _split_bf16(x):
    hi = x.astype(BF16)
    lo = (x - hi.astype(F32)).astype(BF16)
    return hi, lo


def _dot2(x, w):
    hi, lo = _split_bf16(x)
    return (jnp.dot(hi, w, preferred_element_type=F32)
            + jnp.dot(lo, w, preferred_element_type=F32))


def _ada_kernel(c_ref, w_ref, b_ref, o_ref):
    c = c_ref[...]
    s = c * jax.nn.sigmoid(c)
    o_ref[...] = jnp.dot(s, w_ref[...], preferred_element_type=F32,
                         precision=lax.Precision.HIGHEST) + b_ref[...]


def _ada(c, w, b):
    B, D = c.shape
    n = w.shape[1]
    bn = 1024
    return pl.pallas_call(
        _ada_kernel,
        grid=(n // bn,),
        in_specs=[pl.BlockSpec((B, D), lambda j: (0, 0)),
                  pl.BlockSpec((D, bn), lambda j: (0, j)),
                  pl.BlockSpec((1, bn), lambda j: (0, j))],
        out_specs=pl.BlockSpec((B, bn), lambda j: (0, j)),
        out_shape=jax.ShapeDtypeStruct((B, n), F32),
        compiler_params=pltpu.CompilerParams(dimension_semantics=("parallel",),
                                             vmem_limit_bytes=VMEM_LIMIT),
        name="ada",
    )(c, w, b.reshape(1, n))


def _rope128(y, cos, sin_signed, first_half):
    partner = jnp.where(first_half, pltpu.roll(y, LANES - HALF, 1), pltpu.roll(y, HALF, 1))
    return y * cos + partner * sin_signed


def _group_ms(y, gmat):
    return _dot2(y * y, gmat)


def _inproj_kernel(x_ref, sc_ref, sh_ref, n1_ref, pos_ref, inv_ref, sgn_ref, gq_ref, gk_ref,
                   gsq_ref, gsk_ref, gmat_ref, w_ref,
                   dq_ref, dk_ref, dv_ref, sq_ref, iq_ref, sk_ref, ik_ref, sv_ref, iw_ref, gate_ref):
    x = x_ref[0]
    ms = jnp.mean(x * x, axis=-1, keepdims=True)
    h = x * lax.rsqrt(ms + EPS) * n1_ref[...] * (1.0 + sc_ref[0]) + sh_ref[0]
    hb = h.astype(BF16)
    tm = x.shape[0]

    ang = pos_ref[0].astype(F32) * inv_ref[...]
    cos = jnp.cos(ang)
    sin = jnp.sin(ang) * sgn_ref[...]
    lane = lax.broadcasted_iota(I32, (tm, LANES), 1)
    first_half = (lane % HEAD_DIM) < HALF
    gmat = gmat_ref[...]

    def proj(c0, n):
        return jnp.dot(hb, w_ref[:, c0:c0 + n], preferred_element_type=F32)

    def normed_rope(c0, nblk, gain_ref, out_ref, scale):
        for i in range(nblk):
            y = proj(c0 + i * LANES, LANES)
            if gain_ref is not None:
                y = y * lax.rsqrt(_group_ms(y, gmat) + EPS) * gain_ref[:, i * LANES:(i + 1) * LANES]
            y = _rope128(y, cos, sin, first_half)
            if scale != 1.0:
                y = y * scale
            out_ref[0, :, i * LANES:(i + 1) * LANES] = y.astype(out_ref.dtype)

    qscale = HEAD_DIM ** -0.5
    normed_rope(C_DQ, 4, gq_ref, dq_ref, qscale)
    normed_rope(C_DK, 4, gk_ref, dk_ref, 1.0)
    dv = proj(C_DV, 512)
    sv = proj(C_SV, LANES)
    for s in range(tm // ATT_TILE):
        rows = slice(s * ATT_TILE, (s + 1) * ATT_TILE)
        for hd in range(DIFF_HEADS):
            dv_ref[0, hd, s] = dv[rows, hd * LANES:(hd + 1) * LANES].T.astype(dv_ref.dtype)
        sv_ref[0, s] = sv[rows, :].T[:HEAD_DIM].astype(sv_ref.dtype)
    normed_rope(C_SQ, 4, gsq_ref, sq_ref, qscale)
    normed_rope(C_IQ, 4, None, iq_ref, 1.0)
    normed_rope(C_SK, 1, gsk_ref, sk_ref, 1.0)
    normed_rope(C_IK, 1, None, ik_ref, 1.0)
    iw_ref[0] = proj(C_IW, LANES)
    for i in range(4):
        g = proj(C_GATE + i * 512, 512)
        gate_ref[0, :, i * 512:(i + 1) * 512] = jax.nn.sigmoid(g).astype(gate_ref.dtype)


def _inproj(x, sc1, sh1, n1g, pos, consts, w_perm, tm):
    B, T, D = x.shape
    inv128, sgn128, gq, gk, gsq, gsk, gmat = consts
    tok = lambda n, dt: jax.ShapeDtypeStruct((B, T, n), dt)
    blk = lambda n: pl.BlockSpec((1, tm, n), lambda b, i: (b, i, 0))
    cst = lambda a: pl.BlockSpec(a.shape, lambda b, i: (0,) * a.ndim)
    mod = pl.BlockSpec((1, 1, D), lambda b, i: (b, 0, 0))
    n_att, per_step = T // ATT_TILE, tm // ATT_TILE
    dvt_blk = pl.BlockSpec((1, DIFF_HEADS, per_step, LANES, ATT_TILE), lambda b, i: (b, 0, i, 0, 0))
    svt_blk = pl.BlockSpec((1, per_step, HEAD_DIM, ATT_TILE), lambda b, i: (b, i, 0, 0))
    return pl.pallas_call(
        _inproj_kernel,
        grid=(B, T // tm),
        in_specs=[blk(D), mod, mod, cst(n1g), blk(1), cst(inv128), cst(sgn128), cst(gq), cst(gk),
                  cst(gsq), cst(gsk), cst(gmat), cst(w_perm)],
        out_specs=[blk(512), blk(512), dvt_blk, blk(512), blk(512), blk(128), blk(128), svt_blk,
                   blk(128), blk(2048)],
        out_shape=[tok(512, BF16), tok(512, BF16),
                   jax.ShapeDtypeStruct((B, DIFF_HEADS, n_att, LANES, ATT_TILE), BF16),
                   tok(512, BF16), tok(512, BF16), tok(128, BF16), tok(128, BF16),
                   jax.ShapeDtypeStruct((B, n_att, HEAD_DIM, ATT_TILE), BF16),
                   tok(128, F32), tok(2048, BF16)],
        compiler_params=pltpu.CompilerParams(dimension_semantics=("parallel", "parallel"),
                                             vmem_limit_bytes=VMEM_LIMIT),
        name="inproj",
    )(x, sc1, sh1, n1g, pos, inv128, sgn128, gq, gk, gsq, gsk, gmat, w_perm)


def _diff_kernel(q_ref, k_ref, vt_ref, lq1_ref, lk1_ref, lq2_ref, lk2_ref, og_ref, o_ref, *, tq):
    qi = pl.program_id(1)
    lane = lax.broadcasted_iota(I32, (tq, LANES), 1)
    q2 = []
    for h in range(DIFF_HEADS):
        q = q_ref[0, :, h * LANES:(h + 1) * LANES]
        zero = jnp.zeros_like(q)
        q2.append(jnp.concatenate([jnp.where(lane < HEAD_DIM, q, zero),
                                   jnp.where(lane >= HEAD_DIM, q, zero)], axis=0))

    def tile(j, carry, on_diagonal):
        out = []
        for h in range(DIFF_HEADS):
            m, l, acc = carry[h]
            s = _nt(k_ref[0, j, :, h * LANES:(h + 1) * LANES], q2[h])
            if on_diagonal:
                krow = lax.broadcasted_iota(I32, (tq, 2 * tq), 0)
                qcol = lax.broadcasted_iota(I32, (tq, 2 * tq), 1)
                qcol = jnp.where(qcol >= tq, qcol - tq, qcol)
                s = jnp.where(krow // CHUNK <= qcol // CHUNK, s, NEG)
            m_new = jnp.maximum(m, jnp.max(s, axis=0, keepdims=True))
            alpha = jnp.exp(m - m_new)
            p = jnp.exp(s - m_new)
            l = alpha * l + jnp.sum(p, axis=0, keepdims=True)
            acc = alpha * acc + jnp.dot(vt_ref[0, h, j], p.astype(BF16), preferred_element_type=F32)
            out.append((m_new, l, acc))
        return tuple(out)

    init = tuple((jnp.full((1, 2 * tq), NEG, F32), jnp.zeros((1, 2 * tq), F32),
                  jnp.zeros((LANES, 2 * tq), F32)) for _ in range(DIFF_HEADS))
    carry = lax.fori_loop(0, qi, lambda j, c: tile(j, c, False), init)
    res = tile(qi, carry, True)
    lam = (jnp.exp(jnp.sum(lq1_ref[...] * lk1_ref[...], axis=-1, keepdims=True))
           - jnp.exp(jnp.sum(lq2_ref[...] * lk2_ref[...], axis=-1, keepdims=True)) + LAMBDA_INIT)
    for h in range(DIFF_HEADS):
        _, l, acc = res[h]
        o = acc * (1.0 / l)
        o = o[:, :tq] - lam * o[:, tq:]
        ms = jnp.mean(o * o, axis=0, keepdims=True)
        o = o * lax.rsqrt(ms + EPS) * og_ref[...] * (1.0 - LAMBDA_INIT)
        o_ref[0, :, h * LANES:(h + 1) * LANES] = o.T.astype(o_ref.dtype)


def _diff_attention(dq, dk, vt_tiles, lams, out_g):
    B, T, _ = dq.shape
    tq = ATT_TILE
    n = T // tq
    k_tiles = dk.reshape(B, n, tq, DIFF_HEADS * LANES)
    og_cols = jnp.broadcast_to(out_g.reshape(LANES, 1), (LANES, tq))
    width = DIFF_HEADS * LANES
    cst = lambda a: pl.BlockSpec(a.shape, lambda b, i: (0,) * a.ndim)
    return pl.pallas_call(
        functools.partial(_diff_kernel, tq=tq),
        grid=(B, n),
        in_specs=[pl.BlockSpec((1, tq, width), lambda b, i: (b, i, 0)),
                  pl.BlockSpec((1, n, tq, width), lambda b, i: (b, 0, 0, 0)),
                  pl.BlockSpec((1, DIFF_HEADS, n, LANES, tq), lambda b, i: (b, 0, 0, 0, 0)),
                  cst(lams[0]), cst(lams[1]), cst(lams[2]), cst(lams[3]), cst(og_cols)],
        out_specs=pl.BlockSpec((1, tq, width), lambda b, i: (b, i, 0)),
        out_shape=jax.ShapeDtypeStruct((B, T, width), BF16),
        compiler_params=pltpu.CompilerParams(
            dimension_semantics=("parallel", "parallel"), vmem_limit_bytes=VMEM_LIMIT),
        name="diff_attention",
    )(dq, k_tiles, vt_tiles, *lams, og_cols)


def _dsa_kernel(sq_ref, iq_ref, iw_ref, sk_ref, ik_ref, svt_ref, o_ref, key_ref, *, tq, tk, topk, t_bits):
    qi = pl.program_id(1)
    n_kv = (qi * tq + tq + tk - 1) // tk
    lane = lax.broadcasted_iota(I32, (tq, LANES), 1)
    lo_half = lane < HEAD_DIM
    krow = lax.broadcasted_iota(I32, (tk, tq), 0)
    q_chunk = (qi * tq + lax.broadcasted_iota(I32, (tk, tq), 1)) // CHUNK
    idx_scale = (IDX_HEADS ** -0.5) * (HEAD_DIM ** -0.5)
    key_neg_inf = jnp.int32(-2139095041)

    def all_heads(ref):
        out = []
        for p in range(4):
            slab = ref[0, :, p * LANES:(p + 1) * LANES]
            zero = jnp.zeros_like(slab)
            out.append(jnp.where(lo_half, slab, zero))
            out.append(jnp.where(lo_half, zero, slab))
        return jnp.concatenate(out, axis=0)

    def per_head(x, h):
        return x[:, h * tq:(h + 1) * tq]

    iq_all = all_heads(iq_ref)
    iw_t = iw_ref[0].T
    w_all = jnp.concatenate([iw_t[h:h + 1, :] for h in range(IDX_HEADS)], axis=1)

    def score_body(j, carry):
        r = jnp.maximum(_nt(ik_ref[0, j], iq_all), 0.0) * w_all
        score = per_head(r, 0)
        for h in range(1, IDX_HEADS):
            score = score + per_head(r, h)
        score = score * idx_scale + 0.0
        score = jnp.where((j * tk + krow) // CHUNK <= q_chunk, score, -jnp.inf)
        bits = lax.bitcast_convert_type(score, I32)
        key_ref[j] = bits ^ ((bits >> 31) & jnp.int32(0x7FFFFFFF))
        return carry

    lax.fori_loop(0, n_kv, score_body, 0)

    def count(pred_fn):
        def body(j, c):
            hit = jnp.where(pred_fn(key_ref[j], j * tk + krow), 1.0, 0.0)
            parts = [hit[8 * i:8 * i + 8] for i in range(tk // 8)]
            while len(parts) > 1:
                parts = [parts[i] + parts[i + 1] for i in range(0, len(parts), 2)]
            return c + parts[0]
        c = lax.fori_loop(0, n_kv, body, jnp.zeros((8, tq), F32))
        return jnp.sum(c, axis=0, keepdims=True)

    def bit_body(it, t):
        cand = t + jnp.left_shift(jnp.int32(1), 31 - it)
        c = count(lambda kj, _: kj >= cand)
        return jnp.where(c >= float(topk), cand, t)

    thr = lax.fori_loop(0, 32, bit_body, jnp.full((1, tq), jnp.iinfo(jnp.int32).min, I32))
    need = float(topk) - count(lambda kj, _: kj > thr)

    def tie_body(it, jmax):
        cand = jmax + jnp.left_shift(jnp.int32(1), t_bits - 1 - it)
        c = count(lambda kj, kidx: (kj == thr) & (kidx < cand))
        return jnp.where(c < need, cand, jmax)

    ties = count(lambda kj, _: kj == thr)
    some_partial = jnp.max(jnp.where(need < ties, 1.0, 0.0)) > 0.0
    jmax = lax.cond(some_partial,
                    lambda: lax.fori_loop(0, t_bits, tie_body, jnp.zeros((1, tq), I32)),
                    lambda: jnp.full((1, tq), (1 << t_bits) - 1, I32))

    q_all = all_heads(sq_ref)
    width = DSA_HEADS * tq

    def attn_body(j, carry):
        m, l, acc = carry
        kj = key_ref[j]
        sel = ((kj > thr) | ((kj == thr) & (j * tk + krow <= jmax))) & (kj != key_neg_inf)
        sel1 = jnp.where(sel, 1.0, 0.0)
        sel_all = jnp.concatenate([sel1] * DSA_HEADS, axis=1)
        s = jnp.where(sel_all > 0.5, _nt(sk_ref[0, j], q_all), NEG)
        m_new = jnp.maximum(m, jnp.max(s, axis=0, keepdims=True))
        alpha = jnp.exp(m - m_new)
        p = jnp.exp(s - m_new) * sel_all
        l = alpha * l + jnp.sum(p, axis=0, keepdims=True)
        acc = alpha * acc + jnp.dot(svt_ref[0, j], p.astype(BF16), preferred_element_type=F32)
        return m_new, l, acc

    init = (jnp.full((1, width), NEG, F32), jnp.zeros((1, width), F32), jnp.zeros((HEAD_DIM, width), F32))
    _, l, acc = lax.fori_loop(0, n_kv, attn_body, init)
    o_all = acc * (1.0 / l)
    o_t = jnp.concatenate([per_head(o_all, h) for h in range(DSA_HEADS)], axis=0)
    o_ref[0] = o_t.T.astype(o_ref.dtype)


def _dsa_attention(sq, iq, iw, skk, ikk, svt_tiles, tq):
    B, T, _ = sq.shape
    tk = ATT_TILE
    n = T // tk
    topk = min(DSA_TOPK_MAX, T // 4)
    t_bits = max(1, (T - 1).bit_length())
    sk_tiles = skk.reshape(B, n, tk, LANES)
    ik_tiles = ikk.reshape(B, n, tk, LANES)
    qblk = lambda c: pl.BlockSpec((1, tq, c), lambda b, i: (b, i, 0))
    kblk = lambda a: pl.BlockSpec((1,) + a.shape[1:], lambda b, i: (b, 0, 0, 0))
    return pl.pallas_call(
        functools.partial(_dsa_kernel, tq=tq, tk=tk, topk=topk, t_bits=t_bits),
        grid=(B, T // tq),
        in_specs=[qblk(512), qblk(512), qblk(LANES), kblk(sk_tiles), kblk(ik_tiles), kblk(svt_tiles)],
        out_specs=qblk(512),
        out_shape=jax.ShapeDtypeStruct((B, T, DSA_HEADS * HEAD_DIM), BF16),
        scratch_shapes=[pltpu.VMEM((n, tk, tq), I32)],
        compiler_params=pltpu.CompilerParams(dimension_semantics=("parallel", "parallel"),
                                             vmem_limit_bytes=VMEM_LIMIT),
        name="dsa_attention",
    )(sq, iq, iw, sk_tiles, ik_tiles, svt_tiles)


def _top_rows(s, rid, k, payload=None):
    big = float(2 ** 20)
    vals, ids, pay = [], [], []
    for _ in range(k):
        m = jnp.max(s, axis=0, keepdims=True)
        am = jnp.min(jnp.where(s == m, rid, big), axis=0, keepdims=True)
        hit = rid == am
        vals.append(m)
        ids.append(am)
        if payload is not None:
            pay.append(jnp.sum(jnp.where(hit, payload, 0.0), axis=0, keepdims=True))
        s = jnp.where(hit, -jnp.inf, s)
    cat = lambda xs: jnp.concatenate(xs, axis=0)
    return cat(vals), cat(ids), (cat(pay) if payload is not None else None)


def _pair_groups():
    groups = [(0, 0), (0, 8)] + [(a, 0) for a in range(1, 8)]
    return groups, [PEER_TOPK // (a + 1) for a, _ in groups]


def _pair_candidates(v1, i1, v2, i2):
    n = v1.shape[1]
    brow = lax.broadcasted_iota(I32, (8, n), 0)
    groups, limits = _pair_groups()
    cand, cidx, pos = [], [], []
    for (a, b0), lim in zip(groups, limits):
        ok = brow + b0 < lim
        cand.append(jnp.where(ok, v1[a:a + 1] + v2[b0:b0 + 8], -jnp.inf))
        cidx.append(i1[a:a + 1] * float(PEER_NKEYS) + i2[b0:b0 + 8])
        pos.append((brow + (a * PEER_TOPK + b0)).astype(F32))
    cand.append(v1[8:16] + v2[0:1])
    cidx.append(i1[8:16] * float(PEER_NKEYS) + i2[0:1])
    pos.append(((brow + 8) * PEER_TOPK).astype(F32))
    cat = lambda xs: jnp.concatenate(xs, axis=0)
    return cat(cand), cat(cidx), cat(pos)


def _mix_kernel(oa_ref, ob_ref, gate_ref, x_ref, g1_ref, sc_ref, sh_ref, n2_ref, wa_ref, wb_ref, wo_ref,
                wq_ref, sub_ref, x1_ref, h2_ref, eidx_ref, gw_ref, q_ref):
    ya = jnp.dot(oa_ref[0], wa_ref[...], preferred_element_type=F32)
    yb = jnp.dot(ob_ref[0], wb_ref[...], preferred_element_type=F32)
    D = ya.shape[1]
    tm = ya.shape[0]
    merged = gate_ref[0, :, :D].astype(F32) * ya + gate_ref[0, :, D:].astype(F32) * yb
    x1 = x_ref[0] + g1_ref[0] * jnp.dot(merged.astype(BF16), wo_ref[...], preferred_element_type=F32)
    x1_ref[0] = x1
    ms = jnp.mean(x1 * x1, axis=-1, keepdims=True)
    h2 = x1 * lax.rsqrt(ms + EPS) * n2_ref[...] * (1.0 + sc_ref[0]) + sh_ref[0]
    h2_ref[0] = h2
    q_ref[...] = jnp.dot(h2.astype(BF16), wq_ref[...], preferred_element_type=F32).astype(BF16)

    key_id = lax.broadcasted_iota(I32, (PEER_NKEYS, LANES), 0).astype(F32)
    for t in range(tm // LANES):
        cols = slice(t * LANES, (t + 1) * LANES)
        experts, gates_t = [], []
        for h in range(PEER_HEADS):
            tops = []
            for p in range(2):
                hp = 2 * h + p
                s_t = _nt(sub_ref[hp], q_ref[cols, hp * LANES:(hp + 1) * LANES])
                v, i, _ = _top_rows(s_t, key_id, PEER_TOPK)
                tops.append((v, i))
            cand, cidx, pos = _pair_candidates(*tops[0], *tops[1])
            sc, _, e = _top_rows(cand, pos, PEER_TOPK, payload=cidx)
            ex = jnp.exp(sc - sc[0:1])
            g = ex / jnp.sum(ex, axis=0, keepdims=True)
            experts.append(e)
            gates_t.append(g)
        eidx_ref[cols, :] = (jnp.concatenate(experts, axis=0).T * float(SLAB)).astype(I32)
        gw_ref[cols, :] = jnp.concatenate(gates_t, axis=0).T


def _mix(o_a, o_b, gates, x, g1, sc2, sh2, n2g, wa, wb, wo, wq, sub, tm):
    B, T, D = x.shape
    blk = lambda n: pl.BlockSpec((1, tm, n), lambda b, i: (b, i, 0))
    cst = lambda a: pl.BlockSpec(a.shape, lambda b, i: (0,) * a.ndim)
    mod = pl.BlockSpec((1, 1, D), lambda b, i: (b, 0, 0))
    nblk = T // tm
    tblk = pl.BlockSpec((tm, PEER_PAIRS), lambda b, i: (b * nblk + i, 0))
    return pl.pallas_call(
        _mix_kernel,
        grid=(B, nblk),
        in_specs=[blk(512), blk(512), blk(2 * D), blk(D), mod, mod, mod, cst(n2g), cst(wa), cst(wb),
                  cst(wo), cst(wq), cst(sub)],
        out_specs=[blk(D), blk(D), tblk, tblk],
        out_shape=[jax.ShapeDtypeStruct((B, T, D), F32), jax.ShapeDtypeStruct((B, T, D), F32),
                   jax.ShapeDtypeStruct((B * T, PEER_PAIRS), I32),
                   jax.ShapeDtypeStruct((B * T, PEER_PAIRS), F32)],
        scratch_shapes=[pltpu.VMEM((tm, wq.shape[1]), BF16)],
        compiler_params=pltpu.CompilerParams(dimension_semantics=("parallel", "parallel"),
                                             vmem_limit_bytes=VMEM_LIMIT),
        name="mix_route",
    )(o_a, o_b, gates, x, g1, sc2, sh2, n2g, wa, wb, wo, wq, sub)


SLAB = 4
ROW_TILES = 8
GROUP = 32
RING = 4


def _peer_scratch():
    return ([pltpu.SMEM((GROUP, PEER_PAIRS), I32) for _ in range(RING)] + [pltpu.SemaphoreType.DMA((RING,))]
            + [pltpu.VMEM((PEER_PAIRS * SLAB, LANES), I32) for _ in range(2)])


def _pack_table(t):
    n, d = t.shape
    tb = t.astype(BF16).reshape(n, SLAB, 2, LANES).transpose(0, 1, 3, 2)
    return lax.bitcast_convert_type(tb, I32).reshape(n * SLAB, LANES)


def _resident_table_spec(tab):
    return pl.BlockSpec(tab.shape, lambda i: (0, 0), pipeline_mode=pl.Buffered(1))


def _gather_rows(stage_ref, u, tab_ref, w32_ref):
    for k in range(PEER_PAIRS):
        r = pl.multiple_of(stage_ref[u, k], SLAB)
        w32_ref[k * SLAB:(k + 1) * SLAB, :] = tab_ref[pl.ds(r, SLAB), :]


def _staged_token_loop(idx_hbm, stages, sems, tab_ref, compute, bufs, tb):
    step = pl.program_id(0)
    groups_per_step = tb // GROUP
    total = pl.num_programs(0) * groups_per_step
    first = step * groups_per_step

    def copy(g, slot):
        row = pl.multiple_of(jnp.minimum(g, total - 1) * GROUP, GROUP)
        return pltpu.make_async_copy(idx_hbm.at[pl.ds(row, GROUP), :], stages[slot], sems.at[slot])

    @pl.when(step == 0)
    def _():
        for g in range(RING - 1):
            copy(g, g).start()
        copy(0, 0).wait()

    _gather_rows(stages[0], 0, tab_ref, bufs[0])

    def trip(t, carry):
        for r in range(RING):
            g = first + t * RING + r
            for u in range(GROUP):
                nxt = (r, u + 1) if u + 1 < GROUP else ((r + 1) % RING, 0)
                _gather_rows(stages[nxt[0]], nxt[1], tab_ref, bufs[(u + 1) % 2])
                if u == 0:
                    copy(g + 1, (r + 1) % RING).wait()
                    copy(g + RING - 1, (r + RING - 1) % RING).start()
                compute((t * RING + r) * GROUP + u, bufs[u % 2])
        return carry

    lax.fori_loop(0, groups_per_step // RING, trip, 0)

    @pl.when(step == pl.num_programs(0) - 1)
    def _():
        for g in range(1, RING - 1):
            copy(total + g, g % RING).wait()


def _peer_u_kernel(idx_hbm, h2_ref, g_ref, diag_ref, gsum_ref, tab_ref, w_ref, z_ref, *scratch, tb):
    stages, sems, bufs = scratch[:RING], scratch[RING], scratch[RING + 1:]
    diag = diag_ref[...]

    def compute(i, w32_ref):
        wb = pltpu.bitcast(w32_ref[...], BF16)
        x_row = h2_ref[pl.ds(i, 1), :]
        x8 = jnp.concatenate([x_row[:, r * LANES:(r + 1) * LANES] for r in range(ROW_TILES)], axis=0)
        hi, lo = _split_bf16(x8)
        o = _nt(jnp.concatenate([hi, lo], axis=0), wb)
        z_ref[pl.ds(i, 1), :] = jnp.sum((o[:ROW_TILES] + o[ROW_TILES:]) * diag, axis=0, keepdims=True)

    _staged_token_loop(idx_hbm, stages, sems, tab_ref, compute, bufs, tb)
    a = _dot2(z_ref[...], gsum_ref[...])
    gelu = 0.5 * a * (1.0 + lax.erf(a * (2.0 ** -0.5)))
    w_ref[...] = g_ref[...] * gelu


def _peer_v_kernel(idx_hbm, w_ref, x1_ref, g2_ref, diag_ref, gexp_ref, tab_ref, o_ref, wrep_ref, *scratch, tb):
    stages, sems, bufs = scratch[:RING], scratch[RING], scratch[RING + 1:]
    diag = diag_ref[...]
    wrep_ref[...] = _dot2(w_ref[...], gexp_ref[...])
    g2 = g2_ref[0]

    def compute(i, w32_ref):
        wb = pltpu.bitcast(w32_ref[...], BF16)
        a = wrep_ref[pl.ds(i, 1), :] * diag
        hi, lo = _split_bf16(a)
        o = jnp.dot(jnp.concatenate([hi, lo], axis=0), wb, preferred_element_type=F32)
        o8 = o[:ROW_TILES] + o[ROW_TILES:]
        o_row = jnp.concatenate([o8[r:r + 1, :] for r in range(ROW_TILES)], axis=1)
        o_ref[pl.ds(i, 1), :] = x1_ref[pl.ds(i, 1), :] + g2 * o_row

    _staged_token_loop(idx_hbm, stages, sems, tab_ref, compute, bufs, tb)


def _peer_consts():
    c = np.arange(PEER_PAIRS * ROW_TILES)
    diag = (c[None, :] % ROW_TILES == np.arange(ROW_TILES)[:, None]).astype(np.float32)
    gsum = (c[:, None] // ROW_TILES == np.arange(PEER_PAIRS)[None, :]).astype(np.float32)
    return jnp.asarray(diag), jnp.asarray(gsum, BF16), jnp.asarray(gsum.T, BF16)


def _peer_u(idx4, h2, g, utab, tb):
    N, D = h2.shape
    diag, gsum, _ = _peer_consts()
    cst = lambda a: pl.BlockSpec(a.shape, lambda i: (0,) * a.ndim)
    return pl.pallas_call(
        functools.partial(_peer_u_kernel, tb=tb),
        grid=(N // tb,),
        in_specs=[pl.BlockSpec(memory_space=pl.ANY),
                  pl.BlockSpec((tb, D), lambda i: (i, 0)),
                  pl.BlockSpec((tb, PEER_PAIRS), lambda i: (i, 0)),
                  cst(diag), cst(gsum), _resident_table_spec(utab)],
        out_specs=pl.BlockSpec((tb, PEER_PAIRS), lambda i: (i, 0)),
        out_shape=jax.ShapeDtypeStruct((N, PEER_PAIRS), F32),
        scratch_shapes=[pltpu.VMEM((tb, PEER_PAIRS * ROW_TILES), F32)] + _peer_scratch(),
        compiler_params=pltpu.CompilerParams(dimension_semantics=("arbitrary",),
                                             vmem_limit_bytes=VMEM_LIMIT),
        name="peer_u",
    )(idx4, h2, g, diag, gsum, utab)


def _peer_v(idx4, w, x1, g2, vtab, tb, blocks_per_batch):
    N, D = x1.shape
    diag, _, gexp = _peer_consts()
    cst = lambda a: pl.BlockSpec(a.shape, lambda i: (0,) * a.ndim)
    return pl.pallas_call(
        functools.partial(_peer_v_kernel, tb=tb),
        grid=(N // tb,),
        in_specs=[pl.BlockSpec(memory_space=pl.ANY),
                  pl.BlockSpec((tb, PEER_PAIRS), lambda i: (i, 0)),
                  pl.BlockSpec((tb, D), lambda i: (i, 0)),
                  pl.BlockSpec((1, 1, D), lambda i: (i // blocks_per_batch, 0, 0)),
                  cst(diag), cst(gexp), _resident_table_spec(vtab)],
        out_specs=pl.BlockSpec((tb, D), lambda i: (i, 0)),
        out_shape=jax.ShapeDtypeStruct((N, D), F32),
        scratch_shapes=[pltpu.VMEM((tb, PEER_PAIRS * ROW_TILES), F32)] + _peer_scratch(),
        compiler_params=pltpu.CompilerParams(dimension_semantics=("arbitrary",),
                                             vmem_limit_bytes=VMEM_LIMIT),
        name="peer_v",
    )(idx4, w, x1, g2, diag, gexp, vtab)


def _permute_w_in(w):
    o = np.cumsum([0, 512, 512, 512, 512, 64, 64, 512, 64, 8, 2048])
    dq, dk, dv, sq, sk, sv, iq, ik, iw, gate = [w[:, o[i]:o[i + 1]] for i in range(10)]
    pad = jnp.zeros((w.shape[0], LANES - IDX_HEADS), w.dtype)
    return jnp.concatenate([dq, dk, dv, sq, iq, sk, sk, ik, ik, sv, sv, iw, pad, gate], axis=1)


def _rope_consts(diff_q_g, diff_k_g, dsa_q_g, dsa_k_g):
    inv = ROPE_THETA ** (-(jnp.arange(0, HEAD_DIM, 2, dtype=F32) / HEAD_DIM))
    inv128 = jnp.tile(inv, LANES // HALF).reshape(1, LANES)
    sgn = np.where(np.arange(LANES) % HEAD_DIM < HALF, -1.0, 1.0).astype(np.float32).reshape(1, LANES)
    grp = np.arange(LANES) // HEAD_DIM
    gmat = jnp.asarray((grp[:, None] == grp[None, :]).astype(np.float32) / HEAD_DIM, BF16)
    t8 = lambda g: jnp.tile(g, 8).reshape(1, 512)
    return (inv128, jnp.asarray(sgn), t8(diff_q_g), t8(diff_k_g), t8(dsa_q_g),
            jnp.tile(dsa_k_g, 2).reshape(1, LANES), gmat)


def kernel(x, c, positions, w_ada, b_ada, norm1_g, w_in, diff_q_g, diff_k_g, diff_lam_q1, diff_lam_k1, diff_lam_q2, diff_lam_k2, diff_out_g, dsa_q_g, dsa_k_g, w_branch_a, w_branch_b, w_out, norm2_g, peer_w_q, peer_sub_keys, peer_u, peer_v):
    B, T, D = x.shape
    N = B * T
    assert w_ada.shape[0] == 1, "single-layer block"
    assert T % ATT_TILE == 0, "sequence length must be a multiple of the attention tile"
    tm_in = min(512, T)
    tm_mix = ATT_TILE
    tq_dsa = 128
    tb = 256

    mod = _ada(c, w_ada[0], b_ada[0])
    sh1, sc1, g1, sh2, sc2, g2 = [m.reshape(B, 1, D) for m in jnp.split(mod, 6, axis=-1)]

    consts = _rope_consts(diff_q_g[0], diff_k_g[0], dsa_q_g[0], dsa_k_g[0])
    w_perm = _permute_w_in(w_in[0]).astype(BF16)
    dq, dk, dvt, sq, iq, skk, ikk, svt, iw, gates = _inproj(
        x, sc1, sh1, norm1_g, positions.reshape(B, T, 1), consts, w_perm, tm_in)

    lams = [v.reshape(1, HEAD_DIM) for v in (diff_lam_q1[0], diff_lam_k1[0], diff_lam_q2[0], diff_lam_k2[0])]
    o_a = _diff_attention(dq, dk, dvt, lams, diff_out_g[0])
    o_b = _dsa_attention(sq, iq, iw, skk, ikk, svt, tq_dsa)

    sub = peer_sub_keys[0].reshape(2 * PEER_HEADS, PEER_NKEYS, LANES).astype(BF16)
    x1, h2, eidx, gw = _mix(o_a, o_b, gates, x, g1, sc2, sh2, norm2_g,
                            w_branch_a[0].astype(BF16), w_branch_b[0].astype(BF16),
                            w_out[0].astype(BF16), peer_w_q[0].astype(BF16), sub, tm_mix)

    w = _peer_u(eidx, h2.reshape(N, D), gw, _pack_table(peer_u[0]), tb)
    out = _peer_v(eidx, w, x1.reshape(N, D), g2, _pack_table(peer_v[0]), tb, T // tb)
    return out.reshape(B, T, D)
```

```python
import functools
import math

import numpy as np
import jax
import jax.numpy as jnp
from jax import lax
from jax.experimental import pallas as pl
from jax.experimental.pallas import tpu as pltpu

F32 = jnp.float32
BF16 = jnp.bfloat16
I32 = jnp.int32

LANES = 128
HEAD_DIM = 64
HALF = HEAD_DIM // 2
CHUNK = 64
ROPE_THETA = 10000.0
EPS = 1e-6
DIFF_HEADS = 4
DSA_HEADS = 8
IDX_HEADS = 8
DSA_TOPK_MAX = 256
PEER_HEADS = 8
PEER_NKEYS = 128
PEER_TOPK = 16
PEER_PAIRS = PEER_HEADS * PEER_TOPK
LAMBDA_INIT = 0.8 - 0.6 * math.exp(-0.3 * 0)
NEG = -1e30
ATT_TILE = 256
VMEM_LIMIT = 56 * 1024 * 1024

C_DQ, C_DK, C_DV, C_SQ, C_IQ = 0, 512, 1024, 1536, 2048
C_SK, C_IK, C_SV, C_IW, C_GATE = 2560, 2688, 2816, 2944, 3072
N_COLS = C_GATE + 2048


def _nt(a, b):
    return lax.dot_general(a, b, (((1,), (1,)), ((), ())), preferred_element_type=F32)


def _split_bf16(x):
    hi = x.astype(BF16)
    lo = (x - hi.astype(F32)).astype(BF16)
    return hi, lo


def _dot2(x, w):
    hi, lo = _split_bf16(x)
    return (jnp.dot(hi, w, preferred_element_type=F32)
            + jnp.dot(lo, w, preferred_element_type=F32))


def _ada_kernel(c_ref, w_ref, b_ref, o_ref):
    c = c_ref[...]
    s = c * jax.nn.sigmoid(c)
    o_ref[...] = jnp.dot(s, w_ref[...], preferred_element_type=F32,
                         precision=lax.Precision.HIGHEST) + b_ref[...]


def _ada(c, w, b):
    B, D = c.shape
    n = w.shape[1]
    bn = 1024
    return pl.pallas_call(
        _ada_kernel,
        grid=(n // bn,),
        in_specs=[pl.BlockSpec((B, D), lambda j: (0, 0)),
                  pl.BlockSpec((D, bn), lambda j: (0, j)),
                  pl.BlockSpec((1, bn), lambda j: (0, j))],
        out_specs=pl.BlockSpec((B, bn), lambda j: (0, j)),
        out_shape=jax.ShapeDtypeStruct((B, n), F32),
        compiler_params=pltpu.CompilerParams(dimension_semantics=("parallel",),
                                             vmem_limit_bytes=VMEM_LIMIT),
        name="ada",
    )(c, w, b.reshape(1, n))


def _rope128(y, cos, sin_signed, first_half):
    partner = jnp.where(first_half, pltpu.roll(y, LANES - HALF, 1), pltpu.roll(y, HALF, 1))
    return y * cos + partner * sin_signed


def _group_ms(y, gmat):
    return _dot2(y * y, gmat)


def _inproj_kernel(x_ref, sc_ref, sh_ref, n1_ref, pos_ref, inv_ref, sgn_ref, gq_ref, gk_ref,
                   gsq_ref, gsk_ref, gmat_ref, w_ref,
                   dq_ref, dk_ref, dv_ref, sq_ref, iq_ref, sk_ref, ik_ref, sv_ref, iw_ref, gate_ref):
    x = x_ref[0]
    ms = jnp.mean(x * x, axis=-1, keepdims=True)
    h = x * lax.rsqrt(ms + EPS) * n1_ref[...] * (1.0 + sc_ref[0]) + sh_ref[0]
    hb = h.astype(BF16)
    tm = x.shape[0]

    ang = pos_ref[0].astype(F32) * inv_ref[...]
    cos = jnp.cos(ang)
    sin = jnp.sin(ang) * sgn_ref[...]
    lane = lax.broadcasted_iota(I32, (tm, LANES), 1)
    first_half = (lane % HEAD_DIM) < HALF
    gmat = gmat_ref[...]

    def proj(c0, n):
        return jnp.dot(hb, w_ref[:, c0:c0 + n], preferred_element_type=F32)

    def normed_rope(c0, nblk, gain_ref, out_ref, scale):
        for i in range(nblk):
            y = proj(c0 + i * LANES, LANES)
            if gain_ref is not None:
                y = y * lax.rsqrt(_group_ms(y, gmat) + EPS) * gain_ref[:, i * LANES:(i + 1) * LANES]
            y = _rope128(y, cos, sin, first_half)
            if scale != 1.0:
                y = y * scale
            out_ref[0, :, i * LANES:(i + 1) * LANES] = y.astype(out_ref.dtype)

    qscale = HEAD_DIM ** -0.5
    normed_rope(C_DQ, 4, gq_ref, dq_ref, qscale)
    normed_rope(C_DK, 4, gk_ref, dk_ref, 1.0)
    dv = proj(C_DV, 512)
    sv = proj(C_SV, LANES)
    for s in range(tm // ATT_TILE):
        rows = slice(s * ATT_TILE, (s + 1) * ATT_TILE)
        for hd in range(DIFF_HEADS):
            dv_ref[0, hd, s] = dv[rows, hd * LANES:(hd + 1) * LANES].T.astype(dv_ref.dtype)
        sv_ref[0, s] = sv[rows, :].T[:HEAD_DIM].astype(sv_ref.dtype)
    normed_rope(C_SQ, 4, gsq_ref, sq_ref, qscale)
    normed_rope(C_IQ, 4, None, iq_ref, 1.0)
    normed_rope(C_SK, 1, gsk_ref, sk_ref, 1.0)
    normed_rope(C_IK, 1, None, ik_ref, 1.0)
    iw_ref[0] = proj(C_IW, LANES)
    for i in range(4):
        g = proj(C_GATE + i * 512, 512)
        gate_ref[0, :, i * 512:(i + 1) * 512] = jax.nn.sigmoid(g).astype(gate_ref.dtype)


def _inproj(x, sc1, sh1, n1g, pos, consts, w_perm, tm):
    B, T, D = x.shape
    inv128, sgn128, gq, gk, gsq, gsk, gmat = consts
    tok = lambda n, dt: jax.ShapeDtypeStruct((B, T, n), dt)
    blk = lambda n: pl.BlockSpec((1, tm, n), lambda b, i: (b, i, 0))
    cst = lambda a: pl.BlockSpec(a.shape, lambda b, i: (0,) * a.ndim)
    mod = pl.BlockSpec((1, 1, D), lambda b, i: (b, 0, 0))
    n_att, per_step = T // ATT_TILE, tm // ATT_TILE
    dvt_blk = pl.BlockSpec((1, DIFF_HEADS, per_step, LANES, ATT_TILE), lambda b, i: (b, 0, i, 0, 0))
    svt_blk = pl.BlockSpec((1, per_step, HEAD_DIM, ATT_TILE), lambda b, i: (b, i, 0, 0))
    return pl.pallas_call(
        _inproj_kernel,
        grid=(B, T // tm),
        in_specs=[blk(D), mod, mod, cst(n1g), blk(1), cst(inv128), cst(sgn128), cst(gq), cst(gk),
                  cst(gsq), cst(gsk), cst(gmat), cst(w_perm)],
        out_specs=[blk(512), blk(512), dvt_blk, blk(512), blk(512), blk(128), blk(128), svt_blk,
                   blk(128), blk(2048)],
        out_shape=[tok(512, BF16), tok(512, BF16),
                   jax.ShapeDtypeStruct((B, DIFF_HEADS, n_att, LANES, ATT_TILE), BF16),
                   tok(512, BF16), tok(512, BF16), tok(128, BF16), tok(128, BF16),
                   jax.ShapeDtypeStruct((B, n_att, HEAD_DIM, ATT_TILE), BF16),
                   tok(128, F32), tok(2048, BF16)],
        compiler_params=pltpu.CompilerParams(dimension_semantics=("parallel", "parallel"),
                                             vmem_limit_bytes=VMEM_LIMIT),
        name="inproj",
    )(x, sc1, sh1, n1g, pos, inv128, sgn128, gq, gk, gsq, gsk, gmat, w_perm)


def _diff_kernel(q_ref, k_ref, vt_ref, lq1_ref, lk1_ref, lq2_ref, lk2_ref, og_ref, o_ref, *, tq):
    qi = pl.program_id(1)
    lane = lax.broadcasted_iota(I32, (tq, LANES), 1)
    q2 = []
    for h in range(DIFF_HEADS):
        q = q_ref[0, :, h * LANES:(h + 1) * LANES]
        zero = jnp.zeros_like(q)
        q2.append(jnp.concatenate([jnp.where(lane < HEAD_DIM, q, zero),
                                   jnp.where(lane >= HEAD_DIM, q, zero)], axis=0))

    def tile(j, carry, on_diagonal):
        out = []
        for h in range(DIFF_HEADS):
            m, l, acc = carry[h]
            s = _nt(k_ref[0, j, :, h * LANES:(h + 1) * LANES], q2[h])
            if on_diagonal:
                krow = lax.broadcasted_iota(I32, (tq, 2 * tq), 0)
                qcol = lax.broadcasted_iota(I32, (tq, 2 * tq), 1)
                qcol = jnp.where(qcol >= tq, qcol - tq, qcol)
                s = jnp.where(krow // CHUNK <= qcol // CHUNK, s, NEG)
            m_new = jnp.maximum(m, jnp.max(s, axis=0, keepdims=True))
            alpha = jnp.exp(m - m_new)
            p = jnp.exp(s - m_new)
            l = alpha * l + jnp.sum(p, axis=0, keepdims=True)
            acc = alpha * acc + jnp.dot(vt_ref[0, h, j], p.astype(BF16), preferred_element_type=F32)
            out.append((m_new, l, acc))
        return tuple(out)

    init = tuple((jnp.full((1, 2 * tq), NEG, F32), jnp.zeros((1, 2 * tq), F32),
                  jnp.zeros((LANES, 2 * tq), F32)) for _ in range(DIFF_HEADS))
    carry = lax.fori_loop(0, qi, lambda j, c: tile(j, c, False), init)
    res = tile(qi, carry, True)
    lam = (jnp.exp(jnp.sum(lq1_ref[...] * lk1_ref[...], axis=-1, keepdims=True))
           - jnp.exp(jnp.sum(lq2_ref[...] * lk2_ref[...], axis=-1, keepdims=True)) + LAMBDA_INIT)
    for h in range(DIFF_HEADS):
        _, l, acc = res[h]
        o = acc * (1.0 / l)
        o = o[:, :tq] - lam * o[:, tq:]
        ms = jnp.mean(o * o, axis=0, keepdims=True)
        o = o * lax.rsqrt(ms + EPS) * og_ref[...] * (1.0 - LAMBDA_INIT)
        o_ref[0, :, h * LANES:(h + 1) * LANES] = o.T.astype(o_ref.dtype)


def _diff_attention(dq, dk, vt_tiles, lams, out_g):
    B, T, _ = dq.shape
    tq = ATT_TILE
    n = T // tq
    k_tiles = dk.reshape(B, n, tq, DIFF_HEADS * LANES)
    og_cols = jnp.broadcast_to(out_g.reshape(LANES, 1), (LANES, tq))
    width = DIFF_HEADS * LANES
    cst = lambda a: pl.BlockSpec(a.shape, lambda b, i: (0,) * a.ndim)
    return pl.pallas_call(
        functools.partial(_diff_kernel, tq=tq),
        grid=(B, n),
        in_specs=[pl.BlockSpec((1, tq, width), lambda b, i: (b, i, 0)),
                  pl.BlockSpec((1, n, tq, width), lambda b, i: (b, 0, 0, 0)),
                  pl.BlockSpec((1, DIFF_HEADS, n, LANES, tq), lambda b, i: (b, 0, 0, 0, 0)),
                  cst(lams[0]), cst(lams[1]), cst(lams[2]), cst(lams[3]), cst(og_cols)],
        out_specs=pl.BlockSpec((1, tq, width), lambda b, i: (b, i, 0)),
        out_shape=jax.ShapeDtypeStruct((B, T, width), BF16),
        compiler_params=pltpu.CompilerParams(
            dimension_semantics=("parallel", "parallel"), vmem_limit_bytes=VMEM_LIMIT),
        name="diff_attention",
    )(dq, k_tiles, vt_tiles, *lams, og_cols)


def _dsa_kernel(sq_ref, iq_ref, iw_ref, sk_ref, ik_ref, svt_ref, o_ref, key_ref, *, tq, tk, topk, t_bits):
    qi = pl.program_id(1)
    n_kv = (qi * tq + tq + tk - 1) // tk
    lane = lax.broadcasted_iota(I32, (tq, LANES), 1)
    lo_half = lane < HEAD_DIM
    krow = lax.broadcasted_iota(I32, (tk, tq), 0)
    q_chunk = (qi * tq + lax.broadcasted_iota(I32, (tk, tq), 1)) // CHUNK
    idx_scale = (IDX_HEADS ** -0.5) * (HEAD_DIM ** -0.5)
    key_neg_inf = jnp.int32(-2139095041)

    def all_heads(ref):
        out = []
        for p in range(4):
            slab = ref[0, :, p * LANES:(p + 1) * LANES]
            zero = jnp.zeros_like(slab)
            out.append(jnp.where(lo_half, slab, zero))
            out.append(jnp.where(lo_half, zero, slab))
        return jnp.concatenate(out, axis=0)

    def per_head(x, h):
        return x[:, h * tq:(h + 1) * tq]

    iq_all = all_heads(iq_ref)
    iw_t = iw_ref[0].T
    w_all = jnp.concatenate([iw_t[h:h + 1, :] for h in range(IDX_HEADS)], axis=1)

    def score_body(j, carry):
        r = jnp.maximum(_nt(ik_ref[0, j], iq_all), 0.0) * w_all
        score = per_head(r, 0)
        for h in range(1, IDX_HEADS):
            score = score + per_head(r, h)
        score = score * idx_scale + 0.0
        score = jnp.where((j * tk + krow) // CHUNK <= q_chunk, score, -jnp.inf)
        bits = lax.bitcast_convert_type(score, I32)
        key_ref[j] = bits ^ ((bits >> 31) & jnp.int32(0x7FFFFFFF))
        return carry

    lax.fori_loop(0, n_kv, score_body, 0)

    def count(pred_fn):
        def body(j, c):
            hit = jnp.where(pred_fn(key_ref[j], j * tk + krow), 1.0, 0.0)
            parts = [hit[8 * i:8 * i + 8] for i in range(tk // 8)]
            while len(parts) > 1:
                parts = [parts[i] + parts[i + 1] for i in range(0, len(parts), 2)]
            return c + parts[0]
        c = lax.fori_loop(0, n_kv, body, jnp.zeros((8, tq), F32))
        return jnp.sum(c, axis=0, keepdims=True)

    def bit_body(it, t):
        cand = t + jnp.left_shift(jnp.int32(1), 31 - it)
        c = count(lambda kj, _: kj >= cand)
        return jnp.where(c >= float(topk), cand, t)

    thr = lax.fori_loop(0, 32, bit_body, jnp.full((1, tq), jnp.iinfo(jnp.int32).min, I32))
    need = float(topk) - count(lambda kj, _: kj > thr)

    def tie_body(it, jmax):
        cand = jmax + jnp.left_shift(jnp.int32(1), t_bits - 1 - it)
        c = count(lambda kj, kidx: (kj == thr) & (kidx < cand))
        return jnp.where(c < need, cand, jmax)

    ties = count(lambda kj, _: kj == thr)
    some_partial = jnp.max(jnp.where(need < ties, 1.0, 0.0)) > 0.0
    jmax = lax.cond(some_partial,
                    lambda: lax.fori_loop(0, t_bits, tie_body, jnp.zeros((1, tq), I32)),
                    lambda: jnp.full((1, tq), (1 << t_bits) - 1, I32))

    q_all = all_heads(sq_ref)
    width = DSA_HEADS * tq

    def attn_body(j, carry):
        m, l, acc = carry
        kj = key_ref[j]
        sel = ((kj > thr) | ((kj == thr) & (j * tk + krow <= jmax))) & (kj != key_neg_inf)
        sel1 = jnp.where(sel, 1.0, 0.0)
        sel_all = jnp.concatenate([sel1] * DSA_HEADS, axis=1)
        s = jnp.where(sel_all > 0.5, _nt(sk_ref[0, j], q_all), NEG)
        m_new = jnp.maximum(m, jnp.max(s, axis=0, keepdims=True))
        alpha = jnp.exp(m - m_new)
        p = jnp.exp(s - m_new) * sel_all
        l = alpha * l + jnp.sum(p, axis=0, keepdims=True)
        acc = alpha * acc + jnp.dot(svt_ref[0, j], p.astype(BF16), preferred_element_type=F32)
        return m_new, l, acc

    init = (jnp.full((1, width), NEG, F32), jnp.zeros((1, width), F32), jnp.zeros((HEAD_DIM, width), F32))
    _, l, acc = lax.fori_loop(0, n_kv, attn_body, init)
    o_all = acc * (1.0 / l)
    o_t = jnp.concatenate([per_head(o_all, h) for h in range(DSA_HEADS)], axis=0)
    o_ref[0] = o_t.T.astype(o_ref.dtype)


def _dsa_attention(sq, iq, iw, skk, ikk, svt_tiles, tq):
    B, T, _ = sq.shape
    tk = ATT_TILE
    n = T // tk
    topk = min(DSA_TOPK_MAX, T // 4)
    t_bits = max(1, (T - 1).bit_length())
    sk_tiles = skk.reshape(B, n, tk, LANES)
    ik_tiles = ikk.reshape(B, n, tk, LANES)
    qblk = lambda c: pl.BlockSpec((1, tq, c), lambda b, i: (b, i, 0))
    kblk = lambda a: pl.BlockSpec((1,) + a.shape[1:], lambda b, i: (b, 0, 0, 0))
    return pl.pallas_call(
        functools.partial(_dsa_kernel, tq=tq, tk=tk, topk=topk, t_bits=t_bits),
        grid=(B, T // tq),
        in_specs=[qblk(512), qblk(512), qblk(LANES), kblk(sk_tiles), kblk(ik_tiles), kblk(svt_tiles)],
        out_specs=qblk(512),
        out_shape=jax.ShapeDtypeStruct((B, T, DSA_HEADS * HEAD_DIM), BF16),
        scratch_shapes=[pltpu.VMEM((n, tk, tq), I32)],
        compiler_params=pltpu.CompilerParams(dimension_semantics=("parallel", "parallel"),
                                             vmem_limit_bytes=VMEM_LIMIT),
        name="dsa_attention",
    )(sq, iq, iw, sk_tiles, ik_tiles, svt_tiles)


def _top_rows(s, rid, k, payload=None):
    big = float(2 ** 20)
    vals, ids, pay = [], [], []
    for _ in range(k):
        m = jnp.max(s, axis=0, keepdims=True)
        am = jnp.min(jnp.where(s == m, rid, big), axis=0, keepdims=True)
        hit = rid == am
        vals.append(m)
        ids.append(am)
        if payload is not None:
            pay.append(jnp.sum(jnp.where(hit, payload, 0.0), axis=0, keepdims=True))
        s = jnp.where(hit, -jnp.inf, s)
    cat = lambda xs: jnp.concatenate(xs, axis=0)
    return cat(vals), cat(ids), (cat(pay) if payload is not None else None)


def _pair_groups():
    groups = [(0, 0), (0, 8)] + [(a, 0) for a in range(1, 8)]
    return groups, [PEER_TOPK // (a + 1) for a, _ in groups]


def _pair_candidates(v1, i1, v2, i2):
    n = v1.shape[1]
    brow = lax.broadcasted_iota(I32, (8, n), 0)
    groups, limits = _pair_groups()
    cand, cidx, pos = [], [], []
    for (a, b0), lim in zip(groups, limits):
        ok = brow + b0 < lim
        cand.append(jnp.where(ok, v1[a:a + 1] + v2[b0:b0 + 8], -jnp.inf))
        cidx.append(i1[a:a + 1] * float(PEER_NKEYS) + i2[b0:b0 + 8])
        pos.append((brow + (a * PEER_TOPK + b0)).astype(F32))
    cand.append(v1[8:16] + v2[0:1])
    cidx.append(i1[8:16] * float(PEER_NKEYS) + i2[0:1])
    pos.append(((brow + 8) * PEER_TOPK).astype(F32))
    cat = lambda xs: jnp.concatenate(xs, axis=0)
    return cat(cand), cat(cidx), cat(pos)


def _mix_kernel(oa_ref, ob_ref, gate_ref, x_ref, g1_ref, sc_ref, sh_ref, n2_ref, wa_ref, wb_ref, wo_ref,
                wq_ref, sub_ref, x1_ref, h2_ref, eidx_ref, gw_ref, q_ref):
    ya = jnp.dot(oa_ref[0], wa_ref[...], preferred_element_type=F32)
    yb = jnp.dot(ob_ref[0], wb_ref[...], preferred_element_type=F32)
    D = ya.shape[1]
    tm = ya.shape[0]
    merged = gate_ref[0, :, :D].astype(F32) * ya + gate_ref[0, :, D:].astype(F32) * yb
    x1 = x_ref[0] + g1_ref[0] * jnp.dot(merged.astype(BF16), wo_ref[...], preferred_element_type=F32)
    x1_ref[0] = x1
    ms = jnp.mean(x1 * x1, axis=-1, keepdims=True)
    h2 = x1 * lax.rsqrt(ms + EPS) * n2_ref[...] * (1.0 + sc_ref[0]) + sh_ref[0]
    h2_ref[0] = h2
    q_ref[...] = jnp.dot(h2.astype(BF16), wq_ref[...], preferred_element_type=F32).astype(BF16)

    key_id = lax.broadcasted_iota(I32, (PEER_NKEYS, LANES), 0).astype(F32)
    for t in range(tm // LANES):
        cols = slice(t * LANES, (t + 1) * LANES)
        experts, gates_t = [], []
        for h in range(PEER_HEADS):
            tops = []
            for p in range(2):
                hp = 2 * h + p
                s_t = _nt(sub_ref[hp], q_ref[cols, hp * LANES:(hp + 1) * LANES])
                v, i, _ = _top_rows(s_t, key_id, PEER_TOPK)
                tops.append((v, i))
            cand, cidx, pos = _pair_candidates(*tops[0], *tops[1])
            sc, _, e = _top_rows(cand, pos, PEER_TOPK, payload=cidx)
            ex = jnp.exp(sc - sc[0:1])
            g = ex / jnp.sum(ex, axis=0, keepdims=True)
            experts.append(e)
            gates_t.append(g)
        eidx_ref[cols, :] = (jnp.concatenate(experts, axis=0).T * float(SLAB)).astype(I32)
        gw_ref[cols, :] = jnp.concatenate(gates_t, axis=0).T


def _mix(o_a, o_b, gates, x, g1, sc2, sh2, n2g, wa, wb, wo, wq, sub, tm):
    B, T, D = x.shape
    blk = lambda n: pl.BlockSpec((1, tm, n), lambda b, i: (b, i, 0))
    cst = lambda a: pl.BlockSpec(a.shape, lambda b, i: (0,) * a.ndim)
    mod = pl.BlockSpec((1, 1, D), lambda b, i: (b, 0, 0))
    nblk = T // tm
    tblk = pl.BlockSpec((tm, PEER_PAIRS), lambda b, i: (b * nblk + i, 0))
    return pl.pallas_call(
        _mix_kernel,
        grid=(B, nblk),
        in_specs=[blk(512), blk(512), blk(2 * D), blk(D), mod, mod, mod, cst(n2g), cst(wa), cst(wb),
                  cst(wo), cst(wq), cst(sub)],
        out_specs=[blk(D), blk(D), tblk, tblk],
        out_shape=[jax.ShapeDtypeStruct((B, T, D), F32), jax.ShapeDtypeStruct((B, T, D), F32),
                   jax.ShapeDtypeStruct((B * T, PEER_PAIRS), I32),
                   jax.ShapeDtypeStruct((B * T, PEER_PAIRS), F32)],
        scratch_shapes=[pltpu.VMEM((tm, wq.shape[1]), BF16)],
        compiler_params=pltpu.CompilerParams(dimension_semantics=("parallel", "parallel"),
                                             vmem_limit_bytes=VMEM_LIMIT),
        name="mix_route",
    )(o_a, o_b, gates, x, g1, sc2, sh2, n2g, wa, wb, wo, wq, sub)


SLAB = 4
ROW_TILES = 8
GROUP_U = 32
GROUP_V = 8
RING = 4


def _peer_scratch(group):
    return ([pltpu.SMEM((group, PEER_PAIRS), I32) for _ in range(RING)] + [pltpu.SemaphoreType.DMA((RING,))]
            + [pltpu.VMEM((PEER_PAIRS * SLAB, LANES), I32) for _ in range(2)])


def _pack_table(t):
    n, d = t.shape
    tb = t.astype(BF16).reshape(n, SLAB, 2, LANES).transpose(0, 1, 3, 2)
    return lax.bitcast_convert_type(tb, I32).reshape(n * SLAB, LANES)


def _resident_table_spec(tab):
    return pl.BlockSpec(tab.shape, lambda i: (0, 0), pipeline_mode=pl.Buffered(1))


def _gather_rows(stage_ref, u, tab_ref, w32_ref):
    for k in range(PEER_PAIRS):
        r = pl.multiple_of(stage_ref[u, k], SLAB)
        w32_ref[k * SLAB:(k + 1) * SLAB, :] = tab_ref[pl.ds(r, SLAB), :]


def _staged_token_loop(idx_hbm, stages, sems, tab_ref, compute, bufs, tb):
    step = pl.program_id(0)
    GROUP = stages[0].shape[0]
    groups_per_step = tb // GROUP
    total = pl.num_programs(0) * groups_per_step
    first = step * groups_per_step

    def copy(g, slot):
        row = pl.multiple_of(jnp.minimum(g, total - 1) * GROUP, GROUP)
        return pltpu.make_async_copy(idx_hbm.at[pl.ds(row, GROUP), :], stages[slot], sems.at[slot])

    @pl.when(step == 0)
    def _():
        for g in range(RING - 1):
            copy(g, g).start()
        copy(0, 0).wait()

    _gather_rows(stages[0], 0, tab_ref, bufs[0])

    def trip(t, carry):
        for r in range(RING):
            g = first + t * RING + r
            for u in range(GROUP):
                nxt = (r, u + 1) if u + 1 < GROUP else ((r + 1) % RING, 0)
                _gather_rows(stages[nxt[0]], nxt[1], tab_ref, bufs[(u + 1) % 2])
                if u == 0:
                    copy(g + 1, (r + 1) % RING).wait()
                    copy(g + RING - 1, (r + RING - 1) % RING).start()
                compute((t * RING + r) * GROUP + u, bufs[u % 2])
        return carry

    lax.fori_loop(0, groups_per_step // RING, trip, 0)

    @pl.when(step == pl.num_programs(0) - 1)
    def _():
        for g in range(1, RING - 1):
            copy(total + g, g % RING).wait()


def _peer_u_kernel(idx_hbm, h2_ref, g_ref, diag_ref, gsum_ref, tab_ref, w_ref, z_ref, *scratch, tb):
    stages, sems, bufs = scratch[:RING], scratch[RING], scratch[RING + 1:]
    diag = diag_ref[...]

    def compute(i, w32_ref):
        wb = pltpu.bitcast(w32_ref[...], BF16)
        x_row = h2_ref[pl.ds(i, 1), :]
        x8 = jnp.concatenate([x_row[:, r * LANES:(r + 1) * LANES] for r in range(ROW_TILES)], axis=0)
        hi, lo = _split_bf16(x8)
        o = _nt(jnp.concatenate([hi, lo], axis=0), wb)
        z_ref[pl.ds(i, 1), :] = jnp.sum((o[:ROW_TILES] + o[ROW_TILES:]) * diag, axis=0, keepdims=True)

    _staged_token_loop(idx_hbm, stages, sems, tab_ref, compute, bufs, tb)
    a = _dot2(z_ref[...], gsum_ref[...])
    gelu = 0.5 * a * (1.0 + lax.erf(a * (2.0 ** -0.5)))
    w_ref[...] = g_ref[...] * gelu


def _peer_v_kernel(idx_hbm, w_ref, x1_ref, g2_ref, diag_ref, gexp_ref, tab_ref, o_ref, wrep_ref, *scratch, tb):
    stages, sems, bufs = scratch[:RING], scratch[RING], scratch[RING + 1:]
    diag = diag_ref[...]
    wrep_ref[...] = _dot2(w_ref[...], gexp_ref[...])
    g2 = g2_ref[0]

    def compute(i, w32_ref):
        wb = pltpu.bitcast(w32_ref[...], BF16)
        a = wrep_ref[pl.ds(i, 1), :] * diag
        hi, lo = _split_bf16(a)
        o = jnp.dot(jnp.concatenate([hi, lo], axis=0), wb, preferred_element_type=F32)
        o8 = o[:ROW_TILES] + o[ROW_TILES:]
        o_row = jnp.concatenate([o8[r:r + 1, :] for r in range(ROW_TILES)], axis=1)
        o_ref[pl.ds(i, 1), :] = x1_ref[pl.ds(i, 1), :] + g2 * o_row

    _staged_token_loop(idx_hbm, stages, sems, tab_ref, compute, bufs, tb)


def _peer_consts():
    c = np.arange(PEER_PAIRS * ROW_TILES)
    diag = (c[None, :] % ROW_TILES == np.arange(ROW_TILES)[:, None]).astype(np.float32)
    gsum = (c[:, None] // ROW_TILES == np.arange(PEER_PAIRS)[None, :]).astype(np.float32)
    return jnp.asarray(diag), jnp.asarray(gsum, BF16), jnp.asarray(gsum.T, BF16)


def _peer_u(idx4, h2, g, utab, tb):
    N, D = h2.shape
    diag, gsum, _ = _peer_consts()
    cst = lambda a: pl.BlockSpec(a.shape, lambda i: (0,) * a.ndim)
    return pl.pallas_call(
        functools.partial(_peer_u_kernel, tb=tb),
        grid=(N // tb,),
        in_specs=[pl.BlockSpec(memory_space=pl.ANY),
                  pl.BlockSpec((tb, D), lambda i: (i, 0)),
                  pl.BlockSpec((tb, PEER_PAIRS), lambda i: (i, 0)),
                  cst(diag), cst(gsum), _resident_table_spec(utab)],
        out_specs=pl.BlockSpec((tb, PEER_PAIRS), lambda i: (i, 0)),
        out_shape=jax.ShapeDtypeStruct((N, PEER_PAIRS), F32),
        scratch_shapes=[pltpu.VMEM((tb, PEER_PAIRS * ROW_TILES), F32)] + _peer_scratch(GROUP_U),
        compiler_params=pltpu.CompilerParams(dimension_semantics=("arbitrary",),
                                             vmem_limit_bytes=VMEM_LIMIT),
        name="peer_u",
    )(idx4, h2, g, diag, gsum, utab)


def _peer_v(idx4, w, x1, g2, vtab, tb, blocks_per_batch):
    N, D = x1.shape
    diag, _, gexp = _peer_consts()
    cst = lambda a: pl.BlockSpec(a.shape, lambda i: (0,) * a.ndim)
    return pl.pallas_call(
        functools.partial(_peer_v_kernel, tb=tb),
        grid=(N // tb,),
        in_specs=[pl.BlockSpec(memory_space=pl.ANY),
                  pl.BlockSpec((tb, PEER_PAIRS), lambda i: (i, 0)),
                  pl.BlockSpec((tb, D), lambda i: (i, 0)),
                  pl.BlockSpec((1, 1, D), lambda i: (i // blocks_per_batch, 0, 0)),
                  cst(diag), cst(gexp), _resident_table_spec(vtab)],
        out_specs=pl.BlockSpec((tb, D), lambda i: (i, 0)),
        out_shape=jax.ShapeDtypeStruct((N, D), F32),
        scratch_shapes=[pltpu.VMEM((tb, PEER_PAIRS * ROW_TILES), F32)] + _peer_scratch(GROUP_V),
        compiler_params=pltpu.CompilerParams(dimension_semantics=("arbitrary",),
                                             vmem_limit_bytes=VMEM_LIMIT),
        name="peer_v",
    )(idx4, w, x1, g2, diag, gexp, vtab)


def _permute_w_in(w):
    o = np.cumsum([0, 512, 512, 512, 512, 64, 64, 512, 64, 8, 2048])
    dq, dk, dv, sq, sk, sv, iq, ik, iw, gate = [w[:, o[i]:o[i + 1]] for i in range(10)]
    pad = jnp.zeros((w.shape[0], LANES - IDX_HEADS), w.dtype)
    return jnp.concatenate([dq, dk, dv, sq, iq, sk, sk, ik, ik, sv, sv, iw, pad, gate], axis=1)


def _rope_consts(diff_q_g, diff_k_g, dsa_q_g, dsa_k_g):
    inv = ROPE_THETA ** (-(jnp.arange(0, HEAD_DIM, 2, dtype=F32) / HEAD_DIM))
    inv128 = jnp.tile(inv, LANES // HALF).reshape(1, LANES)
    sgn = np.where(np.arange(LANES) % HEAD_DIM < HALF, -1.0, 1.0).astype(np.float32).reshape(1, LANES)
    grp = np.arange(LANES) // HEAD_DIM
    gmat = jnp.asarray((grp[:, None] == grp[None, :]).astype(np.float32) / HEAD_DIM, BF16)
    t8 = lambda g: jnp.tile(g, 8).reshape(1, 512)
    return (inv128, jnp.asarray(sgn), t8(diff_q_g), t8(diff_k_g), t8(dsa_q_g),
            jnp.tile(dsa_k_g, 2).reshape(1, LANES), gmat)


def kernel(x, c, positions, w_ada, b_ada, norm1_g, w_in, diff_q_g, diff_k_g, diff_lam_q1, diff_lam_k1, diff_lam_q2, diff_lam_k2, diff_out_g, dsa_q_g, dsa_k_g, w_branch_a, w_branch_b, w_out, norm2_g, peer_w_q, peer_sub_keys, peer_u, peer_v):
    B, T, D = x.shape
    N = B * T
    assert w_ada.shape[0] == 1, "single-layer block"
    assert T % ATT_TILE == 0, "sequence length must be a multiple of the attention tile"
    tm_in = min(512, T)
    tm_mix = ATT_TILE
    tq_dsa = 128
    tb = 256

    mod = _ada(c, w_ada[0], b_ada[0])
    sh1, sc1, g1, sh2, sc2, g2 = [m.reshape(B, 1, D) for m in jnp.split(mod, 6, axis=-1)]

    consts = _rope_consts(diff_q_g[0], diff_k_g[0], dsa_q_g[0], dsa_k_g[0])
    w_perm = _permute_w_in(w_in[0]).astype(BF16)
    dq, dk, dvt, sq, iq, skk, ikk, svt, iw, gates = _inproj(
        x, sc1, sh1, norm1_g, positions.reshape(B, T, 1), consts, w_perm, tm_in)

    lams = [v.reshape(1, HEAD_DIM) for v in (diff_lam_q1[0], diff_lam_k1[0], diff_lam_q2[0], diff_lam_k2[0])]
    o_a = _diff_attention(dq, dk, dvt, lams, diff_out_g[0])
    o_b = _dsa_attention(sq, iq, iw, skk, ikk, svt, tq_dsa)

    sub = peer_sub_keys[0].reshape(2 * PEER_HEADS, PEER_NKEYS, LANES).astype(BF16)
    x1, h2, eidx, gw = _mix(o_a, o_b, gates, x, g1, sc2, sh2, norm2_g,
                            w_branch_a[0].astype(BF16), w_branch_b[0].astype(BF16),
                            w_out[0].astype(BF16), peer_w_q[0].astype(BF16), sub, tm_mix)

    w = _peer_u(eidx, h2.reshape(N, D), gw, _pack_table(peer_u[0]), tb)
    out = _peer_v(eidx, w, x1.reshape(N, D), g2, _pack_table(peer_v[0]), tb, T // tb)
    return out.reshape(B, T, D)
```
